```python
import math
import jax
import jax.numpy as jnp
from jax import lax
import numpy as np

D_MODEL = 1024
BATCH = 1
SEQ = 16384
DEPTH = 4
DEC_BATCH = 16
DEC_SEQ = 4096
PAST_LEN = 128

HEAD_DIM = 64
A_HEADS = 4
A_KV_HEADS = 2
A_WINDOW = 128
MLA_HEADS = 4
MLA_Q_RANK = 256
MLA_KV_RANK = 128
MLA_NOPE = 64
MLA_ROPE = 32
MLA_V = 64
C_HEADS = 4
C_GROUPS = ((128, 1), (512, 4), (2048, 16))
D_HEADS = 4
D_KV_HEADS = 2
GRID_W = 64
Q_BLOCK = 128
N_BRANCHES = 4
BRANCH_W = 256
N_BUCKETS = 32
MAX_DISTANCE = 1024
N_BIAS_HEADS = A_HEADS + C_HEADS * len(C_GROUPS)
N_EXPERTS = 16
EC_CAPACITY_FACTOR = 2
EXPERT_FF = 1024
ROPE_THETA = 10000.0
EPS = 1e-6
NEG = -1e30
IN_SPLITS = (A_HEADS * HEAD_DIM, A_KV_HEADS * HEAD_DIM, A_KV_HEADS * HEAD_DIM,
             MLA_Q_RANK, MLA_KV_RANK, MLA_ROPE,
             C_HEADS * len(C_GROUPS) * HEAD_DIM, C_HEADS * len(C_GROUPS) * HEAD_DIM, C_HEADS * len(C_GROUPS) * HEAD_DIM,
             D_HEADS * HEAD_DIM, D_KV_HEADS * HEAD_DIM, D_KV_HEADS * HEAD_DIM)
IN_COLS = sum(IN_SPLITS)

kernel_name = 'hybrid_parallel_gated_encoder_ec_moe'


def rmsnorm(x, g):
    xf = x.astype(jnp.float32)
    y = xf * lax.rsqrt(jnp.mean(xf * xf, axis=-1, keepdims=True) + EPS)
    return (y * g.astype(jnp.float32)).astype(x.dtype)


def rope(x, pos):
    half = x.shape[-1] // 2
    inv = ROPE_THETA ** (-jnp.arange(half, dtype=jnp.float32) / half)
    ang = pos.astype(jnp.float32)[:, None] * inv[None, :]
    cos = jnp.cos(ang)[None, :, None, :]
    sin = jnp.sin(ang)[None, :, None, :]
    x1 = x[..., :half].astype(jnp.float32)
    x2 = x[..., half:].astype(jnp.float32)
    return jnp.concatenate([x1 * cos - x2 * sin, x1 * sin + x2 * cos], axis=-1).astype(x.dtype)


def t5_bucket(rel):
    nb = N_BUCKETS // 2
    max_exact = nb // 2
    n = jnp.abs(rel)
    large = max_exact + (jnp.log(jnp.maximum(n, 1).astype(jnp.float32) / max_exact)
                         / math.log(MAX_DISTANCE / max_exact) * (nb - max_exact)).astype(jnp.int32)
    large = jnp.minimum(large, nb - 1)
    return jnp.where(rel > 0, nb, 0) + jnp.where(n < max_exact, n, large)


def t5_bias(rel, table):
    return jnp.moveaxis(table[t5_bucket(rel)].astype(jnp.float32), -1, 0)


def band_rel(w):
    return jnp.arange(3 * w, dtype=jnp.int32)[None, :] - w - jnp.arange(w, dtype=jnp.int32)[:, None]


def banded_attention(q, k, v, w, bias, sink, scale):
    B, M, Hkv, G, dh = q.shape
    dv = v.shape[-1]
    nb = -(-M // w)
    extra = nb * w - M
    qb = jnp.pad(q, ((0, 0), (0, extra), (0, 0), (0, 0), (0, 0))).reshape(B, nb, w, Hkv, G, dh)

    def key_blocks(t):
        tp = jnp.pad(t, ((0, 0), (w, extra + w), (0, 0), (0, 0))).reshape(B, nb + 2, w, Hkv, t.shape[-1])
        return jnp.concatenate([tp[:, :-2], tp[:, 1:-1], tp[:, 2:]], axis=2)

    kb = key_blocks(k)
    vb = key_blocks(v)
    s = jnp.einsum('bnqhgd,bnkhd->bnhgqk', qb, kb).astype(jnp.float32) * scale + bias
    kpos = jnp.arange(nb, dtype=jnp.int32)[:, None, None] * w + (jnp.arange(3 * w, dtype=jnp.int32) - w)[None, None, :]
    valid = (jnp.abs(band_rel(w)) <= w)[None] & (kpos >= 0) & (kpos < M)
    s = jnp.where(valid[None, :, None, None], s, NEG)
    m = jnp.max(s, axis=-1)
    if sink is not None:
        sk = sink.astype(jnp.float32)[None, None, :, :, None]
        m = jnp.maximum(m, sk)
    p = jnp.exp(s - m[..., None])
    denom = jnp.sum(p, axis=-1)
    if sink is not None:
        denom = denom + jnp.exp(sk - m)
    o = jnp.einsum('bnhgqk,bnkhd->bnqhgd', p.astype(v.dtype), vb).astype(jnp.float32)
    o = o / jnp.moveaxis(denom, -1, 2)[..., None]
    o = o.reshape(B, nb * w, Hkv * G, dv)[:, :M].astype(v.dtype)
    lse = jnp.moveaxis(m + jnp.log(denom), -1, 2).reshape(B, nb * w, Hkv * G)[:, :M]
    return o, lse


def dense_blocked_attention(q, k, v, scale):
    B, L, Hkv, G, dq = q.shape
    dv = v.shape[-1]
    nb = L // Q_BLOCK
    qb = q.reshape(B, nb, Q_BLOCK, Hkv, G, dq).transpose(1, 0, 2, 3, 4, 5)

    def one_block(qi):
        s = jnp.einsum('bqhgd,bkhd->bhgqk', qi, k).astype(jnp.float32) * scale
        p = jax.nn.softmax(s, axis=-1)
        return jnp.einsum('bhgqk,bkhd->bqhgd', p.astype(v.dtype), v)

    o = lax.map(one_block, qb)
    return o.transpose(1, 0, 2, 3, 4, 5).reshape(B, L, Hkv * G, dv)


def dilated_group(q, k, v, dil, w, table):
    B, L, H, dh = q.shape
    M = L // dil

    def to_sub(x):
        return x.reshape(B, M, dil, H, x.shape[-1]).transpose(0, 2, 1, 3, 4).reshape(B * dil, M, H, x.shape[-1])

    bias = t5_bias(band_rel(w) * dil, table).reshape(H, 1, w, 3 * w)
    o, lse = banded_attention(to_sub(q)[:, :, :, None], to_sub(k), to_sub(v), w, bias, None, HEAD_DIM ** -0.5)
    o = o.reshape(B, dil, M, H, dh).transpose(0, 2, 1, 3, 4).reshape(B, L, H, dh)
    lse = lse.reshape(B, dil, M, H).transpose(0, 2, 1, 3).reshape(B, L, H)
    return o, lse


def mixer_block(h, t5_table, w_in, a_sink, mla_q_norm, mla_w_uq, mla_kv_norm, mla_w_ukv,
                d_q_norm, d_k_norm, w_gate, w_branch, w_out):
    B, L, _ = h.shape
    t = jnp.arange(L, dtype=jnp.int32)
    rows = L // GRID_W
    row = jnp.repeat(jnp.arange(rows, dtype=jnp.int32), GRID_W)
    col = jnp.tile(jnp.arange(GRID_W, dtype=jnp.int32), rows)
    split_at = [int(i) for i in np.cumsum(IN_SPLITS)[:-1]]
    qa, ka, va, cq, ckv, kr, qc, kc, vc, qd, kd, vd = jnp.split(h @ w_in, split_at, axis=-1)

    ga = A_HEADS // A_KV_HEADS
    bias_a = t5_bias(band_rel(A_WINDOW), t5_table[:, :A_HEADS]).reshape(A_KV_HEADS, ga, A_WINDOW, 3 * A_WINDOW)
    o_a, _ = banded_attention(qa.reshape(B, L, A_KV_HEADS, ga, HEAD_DIM),
                              ka.reshape(B, L, A_KV_HEADS, HEAD_DIM),
                              va.reshape(B, L, A_KV_HEADS, HEAD_DIM),
                              A_WINDOW, bias_a, a_sink.reshape(A_KV_HEADS, ga), HEAD_DIM ** -0.5)

    qm = (rmsnorm(cq, mla_q_norm) @ mla_w_uq).reshape(B, L, MLA_HEADS, MLA_NOPE + MLA_ROPE)
    kvm = (rmsnorm(ckv, mla_kv_norm) @ mla_w_ukv).reshape(B, L, MLA_HEADS, MLA_NOPE + MLA_V)
    k_pe = jnp.broadcast_to(rope(kr[:, :, None, :], t), (B, L, MLA_HEADS, MLA_ROPE))
    q_b = jnp.concatenate([qm[..., :MLA_NOPE], rope(qm[..., MLA_NOPE:], t)], axis=-1)[:, :, :, None]
    k_b = jnp.concatenate([kvm[..., :MLA_NOPE], k_pe], axis=-1)
    o_b = dense_blocked_attention(q_b, k_b, kvm[..., MLA_NOPE:], (MLA_NOPE + MLA_ROPE) ** -0.5)

    nc = C_HEADS * len(C_GROUPS)
    qc = qc.reshape(B, L, nc, HEAD_DIM)
    kc = kc.reshape(B, L, nc, HEAD_DIM)
    vc = vc.reshape(B, L, nc, HEAD_DIM)
    outs, lses = [], []
    for g, (window, dil) in enumerate(C_GROUPS):
        lo, hi = g * C_HEADS, (g + 1) * C_HEADS
        o_g, lse_g = dilated_group(qc[:, :, lo:hi], kc[:, :, lo:hi], vc[:, :, lo:hi], dil, window // (2 * dil),
                                   t5_table[:, A_HEADS + lo:A_HEADS + hi])
        outs.append(o_g)
        lses.append(lse_g)
    wts = jax.nn.softmax(jnp.stack(lses, axis=0), axis=0)
    o_c = jnp.sum(wts[..., None] * jnp.stack(outs, axis=0).astype(jnp.float32), axis=0).astype(h.dtype)

    gd = D_HEADS // D_KV_HEADS

    def qk_norm_axial(x, g):
        x = rmsnorm(x, g)
        return jnp.concatenate([rope(x[..., :HEAD_DIM // 2], row), rope(x[..., HEAD_DIM // 2:], col)], axis=-1)

    q_d = qk_norm_axial(qd.reshape(B, L, D_HEADS, HEAD_DIM), d_q_norm).reshape(B, L, D_KV_HEADS, gd, HEAD_DIM)
    k_d = qk_norm_axial(kd.reshape(B, L, D_KV_HEADS, HEAD_DIM), d_k_norm)
    o_d = dense_blocked_attention(q_d, k_d, vd.reshape(B, L, D_KV_HEADS, HEAD_DIM), HEAD_DIM ** -0.5)

    branches = (o_a, o_b, o_c, o_d)
    merged = jax.nn.sigmoid(h @ w_gate[0]) * (branches[0].reshape(B, L, BRANCH_W) @ w_branch[0])
    for bi in range(1, N_BRANCHES):
        merged = merged + jax.nn.sigmoid(h @ w_gate[bi]) * (branches[bi].reshape(B, L, BRANCH_W) @ w_branch[bi])
    return merged @ w_out


def expert_choice_ffn(h, w_router, w_g, w_u, w_d):
    B, L, D = h.shape
    n = B * L
    xs = h.reshape(n, D)
    cap = EC_CAPACITY_FACTOR * n // N_EXPERTS
    aff = jax.nn.softmax((xs @ w_router).astype(jnp.float32), axis=-1)
    gate, idx = lax.top_k(aff.T, cap)
    xe = xs[idx]
    he = jax.nn.silu(jnp.einsum('ecd,edf->ecf', xe, w_g)) * jnp.einsum('ecd,edf->ecf', xe, w_u)
    ye = jnp.einsum('ecf,efd->ecd', he, w_d) * gate[..., None].astype(h.dtype)
    y = jnp.zeros_like(xs).at[idx.reshape(-1)].add(ye.reshape(-1, D))
    return y.reshape(B, L, D)


def trunk(x, t5_table, norm_mix, w_in, a_sink, mla_q_norm, mla_w_uq, mla_kv_norm, mla_w_ukv,
          d_q_norm, d_k_norm, w_gate, w_branch, w_out, norm_ffn, w_router, w_exp_gate, w_exp_up,
          w_exp_down, norm_final):
    for l in range(DEPTH):
        h = rmsnorm(x, norm_mix[l])
        x = x + mixer_block(h, t5_table, w_in[l], a_sink[l], mla_q_norm[l], mla_w_uq[l], mla_kv_norm[l],
                            mla_w_ukv[l], d_q_norm[l], d_k_norm[l], w_gate[l], w_branch[l], w_out[l])
        h = rmsnorm(x, norm_ffn[l])
        x = x + expert_choice_ffn(h, w_router[l], w_exp_gate[l], w_exp_up[l], w_exp_down[l])
    return rmsnorm(x, norm_final)


def setup_inputs(seed: int = 0) -> dict:
    key = jax.random.key(seed)
    ks = jax.random.split(key, 24)
    f32 = jnp.float32

    def nrm(k, shape, scale):
        return jax.random.normal(k, shape, f32) * scale

    def gain(k, shape):
        return 1.0 + 0.1 * jax.random.normal(k, shape, f32)

    return {
        'x_prompt': nrm(ks[0], (BATCH, SEQ, D_MODEL), 1.0),
        'x_sample': nrm(ks[1], (DEC_BATCH, DEC_SEQ, D_MODEL), 1.0),
        't5_table': nrm(ks[2], (N_BUCKETS, N_BIAS_HEADS), 0.5),
        'norm_mix': gain(ks[3], (DEPTH, D_MODEL)),
        'w_in': nrm(ks[4], (DEPTH, D_MODEL, IN_COLS), D_MODEL ** -0.5),
        'a_sink': nrm(ks[5], (DEPTH, A_HEADS), 0.5),
        'mla_q_norm': gain(ks[6], (DEPTH, MLA_Q_RANK)),
        'mla_w_uq': nrm(ks[7], (DEPTH, MLA_Q_RANK, MLA_HEADS * (MLA_NOPE + MLA_ROPE)), MLA_Q_RANK ** -0.5),
        'mla_kv_norm': gain(ks[8], (DEPTH, MLA_KV_RANK)),
        'mla_w_ukv': nrm(ks[9], (DEPTH, MLA_KV_RANK, MLA_HEADS * (MLA_NOPE + MLA_V)), MLA_KV_RANK ** -0.5),
        'd_q_norm': gain(ks[10], (DEPTH, HEAD_DIM)),
        'd_k_norm': gain(ks[11], (DEPTH, HEAD_DIM)),
        'w_gate': nrm(ks[12], (DEPTH, N_BRANCHES, D_MODEL, D_MODEL), D_MODEL ** -0.5),
        'w_branch': nrm(ks[13], (DEPTH, N_BRANCHES, BRANCH_W, D_MODEL), BRANCH_W ** -0.5),
        'w_out': nrm(ks[14], (DEPTH, D_MODEL, D_MODEL), D_MODEL ** -0.5),
        'norm_ffn': gain(ks[15], (DEPTH, D_MODEL)),
        'w_router': nrm(ks[16], (DEPTH, D_MODEL, N_EXPERTS), D_MODEL ** -0.5),
        'w_exp_gate': nrm(ks[17], (DEPTH, N_EXPERTS, D_MODEL, EXPERT_FF), D_MODEL ** -0.5),
        'w_exp_up': nrm(ks[18], (DEPTH, N_EXPERTS, D_MODEL, EXPERT_FF), D_MODEL ** -0.5),
        'w_exp_down': nrm(ks[19], (DEPTH, N_EXPERTS, EXPERT_FF, D_MODEL), EXPERT_FF ** -0.5),
        'norm_final': gain(ks[20], (D_MODEL,)),
    }


def reference(x_prompt, x_sample, t5_table, norm_mix, w_in, a_sink, mla_q_norm, mla_w_uq, mla_kv_norm,
              mla_w_ukv, d_q_norm, d_k_norm, w_gate, w_branch, w_out, norm_ffn, w_router, w_exp_gate,
              w_exp_up, w_exp_down, norm_final):
    y_prompt = trunk(x_prompt, t5_table, norm_mix, w_in, a_sink, mla_q_norm, mla_w_uq, mla_kv_norm, mla_w_ukv,
                     d_q_norm, d_k_norm, w_gate, w_branch, w_out, norm_ffn, w_router, w_exp_gate, w_exp_up,
                     w_exp_down, norm_final)
    y_sample = trunk(x_sample, t5_table, norm_mix, w_in, a_sink, mla_q_norm, mla_w_uq, mla_kv_norm, mla_w_ukv,
                     d_q_norm, d_k_norm, w_gate, w_branch, w_out, norm_ffn, w_router, w_exp_gate, w_exp_up,
                     w_exp_down, norm_final)
    return (y_prompt, y_sample)
```

```python
import functools
import math

import numpy as np
import jax
import jax.numpy as jnp
from jax import lax
from jax.experimental import pallas as pl
from jax.experimental.pallas import tpu as pltpu

F32 = jnp.float32
BF16 = jnp.bfloat16

D_MODEL = 1024
HEAD_DIM = 64
A_HEADS = 4
A_WINDOW = 128
MLA_HEADS = 4
MLA_Q_RANK = 256
MLA_KV_RANK = 128
MLA_NOPE = 64
MLA_ROPE = 32
MLA_V = 64
C_HEADS = 4
C_GROUPS = ((128, 1), (512, 4), (2048, 16))
C_BAND = 64
GRID_W = 64
N_BRANCHES = 4
BRANCH_W = 256
N_BUCKETS = 32
MAX_DISTANCE = 1024
N_EXPERTS = 16
EC_CAPACITY_FACTOR = 2
ROPE_THETA = 10000.0
ROPE_HALF = 16
EPS = 1e-6
NEG = -1e30

LANE = 128
VMEM_LIMIT = 56 * 1024 * 1024

SEG_QA, SEG_KA, SEG_VA = 0, 256, 384
SEG_CQ, SEG_CKV, SEG_KR = 512, 768, 896
SEG_QC, SEG_KC, SEG_VC = 1024, 1792, 2560
SEG_QD, SEG_KD, SEG_VD = 3328, 3584, 3712
IN_COLS_PACKED = 3840

A_SCALE = HEAD_DIM ** -0.5
B_SCALE = (MLA_NOPE + MLA_ROPE) ** -0.5

TILE_PROJ = 512
TILE_MERGE = 256
TILE_Q = 256
TILE_KV = 512
TILE_EXPERT = 512


def _dot(a, b):
    return jnp.dot(a, b, preferred_element_type=F32)


def _dot_nt(a, b):
    return lax.dot_general(a, b, (((1,), (1,)), ((), ())), preferred_element_type=F32)


def _params(*sem):
    return pltpu.CompilerParams(dimension_semantics=sem, vmem_limit_bytes=VMEM_LIMIT)


def _const_spec(shape):
    zeros = (0,) * len(shape)
    return pl.BlockSpec(shape, lambda *_: zeros)


def _smem_spec():
    return pl.BlockSpec(memory_space=pltpu.SMEM)


def _lane_iota(shape):
    return lax.broadcasted_iota(jnp.int32, shape, len(shape) - 1)


def _rope128(x, cos, sin_signed, first_half):
    rot = jnp.where(first_half, pltpu.roll(x, LANE - ROPE_HALF, 1), pltpu.roll(x, ROPE_HALF, 1))
    return x * cos + rot * sin_signed


def _rms(x, gain):
    return x * lax.rsqrt(jnp.mean(x * x, axis=-1, keepdims=True) + EPS) * gain


def _proj_kernel(x_ref, gmix_ref, win_ref, wuq_ref, wukvk_ref, wukvv_ref, gq_ref, gkv_ref, gdq_ref, gdk_ref,
                 g64_ref, cosb_ref, sinb_ref, cosd_ref, sind_ref,
                 h_ref, qa_ref, ka_ref, va_ref, qb_ref, kb_ref, vb_ref, qc_ref, kc_ref, vc_ref,
                 qd_ref, kd_ref, vd_ref):
    h = _rms(x_ref[...], gmix_ref[...]).astype(BF16)
    h_ref[...] = h

    def seg(lo, width):
        return _dot(h, win_ref[:, lo:lo + width])

    first_half = jnp.bitwise_and(_lane_iota((1, LANE)), 2 * ROPE_HALF - 1) < ROPE_HALF

    qa_ref[...] = (seg(SEG_QA, 256) * A_SCALE).astype(BF16)
    ka_ref[...] = seg(SEG_KA, 128).astype(BF16)
    va_ref[...] = seg(SEG_VA, 128).astype(BF16)

    cosb, sinb = cosb_ref[...], sinb_ref[...]
    cqn = _rms(seg(SEG_CQ, MLA_Q_RANK), gq_ref[...]).astype(BF16)
    qm = _dot(cqn, wuq_ref[...])
    for hh in range(MLA_HEADS):
        blk = slice(hh * LANE, (hh + 1) * LANE)
        qb_ref[:, blk] = (_rope128(qm[:, blk], cosb, sinb, first_half) * B_SCALE).astype(BF16)
    ckvn = _rms(seg(SEG_CKV, MLA_KV_RANK), gkv_ref[...]).astype(BF16)
    k_pe = _rope128(seg(SEG_KR, LANE), cosb, sinb, first_half)
    kn = _dot(ckvn, wukvk_ref[...])
    for hh in range(MLA_HEADS):
        blk = slice(hh * LANE, (hh + 1) * LANE)
        kb_ref[:, blk] = (kn[:, blk] + k_pe).astype(BF16)
    vb_ref[...] = _dot(ckvn, wukvv_ref[...]).astype(BF16)

    qc_ref[...] = (seg(SEG_QC, 768) * A_SCALE).astype(BF16)
    kc_ref[...] = seg(SEG_KC, 768).astype(BF16)
    vc_ref[...] = seg(SEG_VC, 768).astype(BF16)

    cosd, sind = cosd_ref[...], sind_ref[...]
    g64 = g64_ref[...]

    def head_norm(z, gain):
        sq = z * z
        hi = sq.astype(BF16)
        lo = (sq - hi.astype(F32)).astype(BF16)
        ms = _dot(hi, g64) + _dot(lo, g64)
        return z * lax.rsqrt(ms + EPS) * gain

    zq = seg(SEG_QD, 256)
    for bb in range(2):
        blk = slice(bb * LANE, (bb + 1) * LANE)
        y = head_norm(zq[:, blk], gdq_ref[...])
        qd_ref[:, blk] = (_rope128(y, cosd, sind, first_half) * A_SCALE).astype(BF16)
    kd_ref[...] = _rope128(head_norm(seg(SEG_KD, 128), gdk_ref[...]), cosd, sind, first_half).astype(BF16)
    vd_ref[...] = seg(SEG_VD, 128).astype(BF16)


def _proj(x, seq_len, lw, tabs):
    n = x.shape[0]
    t = TILE_PROJ
    per_seq = seq_len // t
    row = lambda w: pl.BlockSpec((t, w), lambda i: (i, 0))
    tab = pl.BlockSpec((t, LANE), lambda i: (i % per_seq, 0))
    widths = (D_MODEL, 256, 128, 128, 512, 512, 256, 768, 768, 768, 256, 128, 128)
    return pl.pallas_call(
        _proj_kernel,
        grid=(n // t,),
        in_specs=[row(D_MODEL), _const_spec((1, D_MODEL)), _const_spec((D_MODEL, IN_COLS_PACKED)),
                  _const_spec((MLA_Q_RANK, 512)), _const_spec((MLA_KV_RANK, 512)), _const_spec((MLA_KV_RANK, 256)),
                  _const_spec((1, MLA_Q_RANK)), _const_spec((1, MLA_KV_RANK)), _const_spec((1, LANE)),
                  _const_spec((1, LANE)), _const_spec((LANE, LANE)), tab, tab, tab, tab],
        out_specs=[row(w) for w in widths],
        out_shape=[jax.ShapeDtypeStruct((n, w), BF16) for w in widths],
        compiler_params=_params("arbitrary"),
        name="norm_in_proj",
    )(x, lw["g_mix"], lw["w_in"], lw["w_uq"], lw["w_ukv_k"], lw["w_ukv_v"], lw["g_q"], lw["g_kv"], lw["g_dq"],
      lw["g_dk"], tabs["g64"], tabs["cos_b"], tabs["sin_b"], tabs["cos_d"], tabs["sin_d"])


def _build_bias(tab_ref, bkt_ref, bias_ref, n_heads, band):
    bkt = bkt_ref[...]
    row = lax.broadcasted_iota(jnp.int32, bkt.shape, 0)
    col = lax.broadcasted_iota(jnp.int32, bkt.shape, 1)
    valid = jnp.abs(col - band - row) <= band
    for hh in range(n_heads):
        def body(bb, acc, hh=hh):
            return jnp.where(bkt == bb, tab_ref[bb, hh], acc)
        acc = lax.fori_loop(0, N_BUCKETS, body, jnp.zeros(bkt.shape, F32))
        bias_ref[hh] = jnp.where(valid, acc, NEG)


def _attn_a_kernel(tab_ref, sink_ref, bkt_ref, q_ref, kp_ref, kc_ref, kn_ref, vp_ref, vc_ref, vn_ref,
                   o_ref, bias_ref):
    n = pl.program_id(1)
    last = pl.num_programs(1) - 1

    @pl.when((pl.program_id(0) == 0) & (n == 0))
    def _():
        _build_bias(tab_ref, bkt_ref, bias_ref, A_HEADS, A_WINDOW)

    w = A_WINDOW
    kcat = jnp.concatenate([kp_ref[...], kc_ref[...], kn_ref[...]], axis=0)
    vcat = jnp.concatenate([vp_ref[...], vc_ref[...], vn_ref[...]], axis=0)
    col = _lane_iota((w, 3 * w))
    edge = (col >= jnp.where(n > 0, 0, w)) & (col < jnp.where(n < last, 3 * w, 2 * w))
    is_lo = _lane_iota((1, LANE)) < HEAD_DIM
    q = q_ref[...]
    zero = jnp.zeros((w, LANE), BF16)
    for blk in range(2):
        qb = q[:, blk * LANE:(blk + 1) * LANE]
        res = []
        for half in range(2):
            head = blk + 2 * half
            qm = jnp.where(is_lo if half == 0 else jnp.logical_not(is_lo), qb, zero)
            s = jnp.where(edge, _dot_nt(qm, kcat) + bias_ref[head], NEG)
            sk = sink_ref[head]
            m = jnp.maximum(jnp.max(s, axis=-1, keepdims=True), sk)
            p = jnp.exp(s - m)
            denom = jnp.sum(p, axis=-1, keepdims=True) + jnp.exp(sk - m)
            res.append(_dot(p.astype(BF16), vcat) / denom)
        o_ref[:, blk * LANE:(blk + 1) * LANE] = jnp.where(is_lo, res[0], res[1]).astype(BF16)


def _attn_a(q, k, v, table, sink, bkt):
    b, l, _ = q.shape
    w = A_WINDOW
    nb = l // w
    kv = lambda f: pl.BlockSpec((None, w, LANE), lambda bi, ni: (bi, f(ni), 0))
    prev, cur, nxt = (lambda ni: jnp.maximum(ni - 1, 0)), (lambda ni: ni), (lambda ni: jnp.minimum(ni + 1, nb - 1))
    return pl.pallas_call(
        _attn_a_kernel,
        grid=(b, nb),
        in_specs=[_smem_spec(), _smem_spec(), _const_spec((w, 3 * w)),
                  pl.BlockSpec((None, w, 256), lambda bi, ni: (bi, ni, 0)),
                  kv(prev), kv(cur), kv(nxt), kv(prev), kv(cur), kv(nxt)],
        out_specs=pl.BlockSpec((None, w, 256), lambda bi, ni: (bi, ni, 0)),
        out_shape=jax.ShapeDtypeStruct((b, l, 256), BF16),
        scratch_shapes=[pltpu.VMEM((A_HEADS, w, 3 * w), F32)],
        compiler_params=_params("arbitrary", "arbitrary"),
        name="attn_window_sink",
    )(table, sink, bkt, q, k, k, k, v, v, v)


def _attn_c_kernel(tab_ref, bkt_ref, q_ref, kl_ref, kc_ref, kr_ref, vl_ref, vc_ref, vr_ref,
                   o_ref, lse_ref, bias_ref):
    n = pl.program_id(1)
    last = pl.num_programs(1) - 1

    @pl.when((pl.program_id(0) == 0) & (n == 0))
    def _():
        _build_bias(tab_ref, bkt_ref, bias_ref, C_HEADS, C_BAND)

    t = 2 * C_BAND
    col = _lane_iota((t, 2 * t))
    edge = (col >= jnp.where(n > 0, 0, C_BAND)) & (col < jnp.where(n < last, 2 * t, t + C_BAND))
    is_lo = _lane_iota((1, LANE)) < HEAD_DIM
    q = q_ref[...]
    zero = jnp.zeros((t, LANE), BF16)
    for blk in range(2):
        lanes = slice(blk * LANE, (blk + 1) * LANE)
        qb = q[:, lanes]
        kwin = jnp.concatenate([kl_ref[:, lanes], kc_ref[:, lanes], kr_ref[:, lanes]], axis=0)
        vwin = jnp.concatenate([vl_ref[:, lanes], vc_ref[:, lanes], vr_ref[:, lanes]], axis=0)
        outs, lses = [], []
        for half in range(2):
            head = 2 * blk + half
            qm = jnp.where(is_lo if half == 0 else jnp.logical_not(is_lo), qb, zero)
            s = jnp.where(edge, _dot_nt(qm, kwin) + bias_ref[head], NEG)
            m = jnp.max(s, axis=-1, keepdims=True)
            p = jnp.exp(s - m)
            denom = jnp.sum(p, axis=-1, keepdims=True)
            outs.append(_dot(p.astype(BF16), vwin) / denom)
            lses.append(m + jnp.log(denom))
        o_ref[:, lanes] = jnp.where(is_lo, outs[0], outs[1]).astype(BF16)
        lse_ref[:, lanes] = jnp.where(is_lo, lses[0], lses[1])


def _attn_c(q, k, v, col_blk, table, bkt):
    s, m, _ = q.shape
    t = 2 * C_BAND
    nb = m // t
    nh = m // C_BAND
    halo = lambda f: pl.BlockSpec((None, C_BAND, 256), lambda si, ni: (si, f(ni), col_blk))
    left, right = (lambda ni: jnp.maximum(2 * ni - 1, 0)), (lambda ni: jnp.minimum(2 * ni + 2, nh - 1))
    mid = pl.BlockSpec((None, t, 256), lambda si, ni: (si, ni, col_blk))
    out = pl.BlockSpec((None, t, 256), lambda si, ni: (si, ni, 0))
    return pl.pallas_call(
        _attn_c_kernel,
        grid=(s, nb),
        in_specs=[_smem_spec(), _const_spec((t, 2 * t)), mid, halo(left), mid, halo(right), halo(left), mid,
                  halo(right)],
        out_specs=[out, out],
        out_shape=[jax.ShapeDtypeStruct((s, m, 256), BF16), jax.ShapeDtypeStruct((s, m, 256), F32)],
        scratch_shapes=[pltpu.VMEM((C_HEADS, t, 2 * t), F32)],
        compiler_params=_params("arbitrary", "arbitrary"),
        name="attn_dilated",
    )(table, bkt, q, k, k, k, v, v, v)


def _flash_pair(q_lo, q_hi, klo_ref, khi_ref, v_ref, o_ref, acc_ref):
    tq = q_lo.shape[0]
    n_chunks = v_ref.shape[0] // TILE_KV
    acc_ref[...] = jnp.zeros(acc_ref.shape, F32)

    def step(c, carry):
        start = pl.multiple_of(c * TILE_KV, TILE_KV)
        vc = v_ref[pl.ds(start, TILE_KV), :]
        out = []
        for i, (q, k_ref) in enumerate(((q_lo, klo_ref), (q_hi, khi_ref))):
            m, l = carry[2 * i], carry[2 * i + 1]
            s = _dot_nt(q, k_ref[pl.ds(start, TILE_KV), :])
            m_new = jnp.maximum(m, jnp.max(s, axis=-1, keepdims=True))
            alpha = jnp.exp(m - m_new)
            p = jnp.exp(s - m_new)
            out += [m_new, alpha * l + jnp.sum(p, axis=-1, keepdims=True)]
            acc_ref[i] = alpha * acc_ref[i] + _dot(p.astype(BF16), vc)
        return tuple(out)

    m0 = jnp.full((tq, 1), NEG, F32)
    l0 = jnp.zeros((tq, 1), F32)
    _, l_lo, _, l_hi = lax.fori_loop(0, n_chunks, step, (m0, l0, m0, l0))
    is_lo = _lane_iota((1, LANE)) < HEAD_DIM
    o_ref[...] = jnp.where(is_lo, acc_ref[0] / l_lo, acc_ref[1] / l_hi).astype(BF16)


def _flash_b_kernel(qlo_ref, qhi_ref, klo_ref, khi_ref, v_ref, o_ref, acc_ref):
    _flash_pair(qlo_ref[...], qhi_ref[...], klo_ref, khi_ref, v_ref, o_ref, acc_ref)


def _flash_d_kernel(q_ref, k_ref, v_ref, o_ref, acc_ref):
    q = q_ref[...]
    is_lo = _lane_iota((1, LANE)) < HEAD_DIM
    zero = jnp.zeros(q.shape, BF16)
    _flash_pair(jnp.where(is_lo, q, zero), jnp.where(is_lo, zero, q), k_ref, k_ref, v_ref, o_ref, acc_ref)


def _flash_b(q, k, v):
    b, l, _ = q.shape
    qs = lambda off: pl.BlockSpec((None, TILE_Q, LANE), lambda bi, pi, qi: (bi, qi, 2 * pi + off))
    ks = lambda off: pl.BlockSpec((None, l, LANE), lambda bi, pi, qi: (bi, 0, 2 * pi + off))
    return pl.pallas_call(
        _flash_b_kernel,
        grid=(b, 2, l // TILE_Q),
        in_specs=[qs(0), qs(1), ks(0), ks(1), pl.BlockSpec((None, l, LANE), lambda bi, pi, qi: (bi, 0, pi))],
        out_specs=pl.BlockSpec((None, TILE_Q, LANE), lambda bi, pi, qi: (bi, qi, pi)),
        out_shape=jax.ShapeDtypeStruct((b, l, 256), BF16),
        scratch_shapes=[pltpu.VMEM((2, TILE_Q, LANE), F32)],
        compiler_params=_params("arbitrary", "arbitrary", "arbitrary"),
        name="attn_dense_mla",
    )(q, q, k, k, v)


def _flash_d(q, k, v):
    b, l, _ = q.shape
    full = pl.BlockSpec((None, l, LANE), lambda bi, pi, qi: (bi, 0, 0))
    return pl.pallas_call(
        _flash_d_kernel,
        grid=(b, 2, l // TILE_Q),
        in_specs=[pl.BlockSpec((None, TILE_Q, LANE), lambda bi, pi, qi: (bi, qi, pi)), full, full],
        out_specs=pl.BlockSpec((None, TILE_Q, LANE), lambda bi, pi, qi: (bi, qi, pi)),
        out_shape=jax.ShapeDtypeStruct((b, l, 256), BF16),
        scratch_shapes=[pltpu.VMEM((2, TILE_Q, LANE), F32)],
        compiler_params=_params("arbitrary", "arbitrary", "arbitrary"),
        name="attn_dense_axial",
    )(q, k, v)


def _merge_kernel(x_ref, h_ref, oa_ref, ob_ref, oc0_ref, oc1_ref, oc2_ref, l0_ref, l1_ref, l2_ref, od_ref,
                  wg_ref, wb_ref, wo_ref, gffn_ref, wrh_ref, wrl_ref, xo_ref, h2_ref, aff_ref):
    h = h_ref[...]
    l0, l1, l2 = l0_ref[...], l1_ref[...], l2_ref[...]
    mx = jnp.maximum(jnp.maximum(l0, l1), l2)
    e0, e1, e2 = jnp.exp(l0 - mx), jnp.exp(l1 - mx), jnp.exp(l2 - mx)
    den = e0 + e1 + e2
    oc = ((e0 / den) * oc0_ref[...].astype(F32) + (e1 / den) * oc1_ref[...].astype(F32)
          + (e2 / den) * oc2_ref[...].astype(F32)).astype(BF16)
    branches = (oa_ref[...], ob_ref[...], oc, od_ref[...])
    merged = None
    for bi in range(N_BRANCHES):
        term = jax.nn.sigmoid(_dot(h, wg_ref[bi])) * _dot(branches[bi], wb_ref[bi])
        merged = term if merged is None else merged + term
    xn = x_ref[...] + _dot(merged.astype(BF16), wo_ref[...])
    xo_ref[...] = xn
    hn = _rms(xn, gffn_ref[...])
    hi = hn.astype(BF16)
    h2_ref[...] = hi
    lo = (hn - hi.astype(F32)).astype(BF16)
    logits = _dot(hi, wrh_ref[...]) + _dot(lo, wrh_ref[...]) + _dot(hi, wrl_ref[...])
    logits = jnp.where(_lane_iota((1, LANE)) < N_EXPERTS, logits, NEG)
    e = jnp.exp(logits - jnp.max(logits, axis=-1, keepdims=True))
    aff_ref[...] = e / jnp.sum(e, axis=-1, keepdims=True)


def _merge(x, h, oa, ob, oc, lse, od, lw):
    n = x.shape[0]
    t = TILE_MERGE
    row = lambda w: pl.BlockSpec((t, w), lambda i: (i, 0))
    return pl.pallas_call(
        _merge_kernel,
        grid=(n // t,),
        in_specs=[row(D_MODEL), row(D_MODEL)] + [row(BRANCH_W)] * 9
                 + [_const_spec((N_BRANCHES, D_MODEL, D_MODEL)), _const_spec((N_BRANCHES, BRANCH_W, D_MODEL)),
                    _const_spec((D_MODEL, D_MODEL)), _const_spec((1, D_MODEL)), _const_spec((D_MODEL, LANE)),
                    _const_spec((D_MODEL, LANE))],
        out_specs=[row(D_MODEL), row(D_MODEL), row(LANE)],
        out_shape=[jax.ShapeDtypeStruct((n, D_MODEL), F32), jax.ShapeDtypeStruct((n, D_MODEL), BF16),
                   jax.ShapeDtypeStruct((n, LANE), F32)],
        compiler_params=_params("arbitrary"),
        name="gated_merge_router",
    )(x, h, oa, ob, oc[0], oc[1], oc[2], lse[0], lse[1], lse[2], od,
      lw["w_gate"], lw["w_branch"], lw["w_out"], lw["g_ffn"], lw["w_router_hi"], lw["w_router_lo"])


def _expert_kernel(xe_ref, gate_ref, wg_ref, wu_ref, wd_ref, ye_ref):
    xe = xe_ref[...]
    he = (jax.nn.silu(_dot(xe, wg_ref[...])) * _dot(xe, wu_ref[...])).astype(BF16)
    ye_ref[...] = _dot(he, wd_ref[...]) * gate_ref[...]


def _experts(xe, gate, lw):
    e, cap, d = xe.shape
    t = min(TILE_EXPERT, cap)
    ff = lw["w_exp_gate"].shape[-1]
    wspec = lambda a, b: pl.BlockSpec((None, a, b), lambda ei, ti: (ei, 0, 0))
    return pl.pallas_call(
        _expert_kernel,
        grid=(e, cap // t),
        in_specs=[pl.BlockSpec((None, t, d), lambda ei, ti: (ei, ti, 0)),
                  pl.BlockSpec((None, t, 1), lambda ei, ti: (ei, ti, 0)),
                  wspec(d, ff), wspec(d, ff), wspec(ff, d)],
        out_specs=pl.BlockSpec((None, t, d), lambda ei, ti: (ei, ti, 0)),
        out_shape=jax.ShapeDtypeStruct((e, cap, d), F32),
        compiler_params=_params("arbitrary", "arbitrary"),
        name="expert_swiglu",
    )(xe, gate, lw["w_exp_gate"], lw["w_exp_up"], lw["w_exp_down"])


def _final_kernel(x_ref, g_ref, o_ref):
    o_ref[...] = _rms(x_ref[...], g_ref[...])


def _final_norm(x, g):
    n = x.shape[0]
    t = TILE_PROJ
    return pl.pallas_call(
        _final_kernel,
        grid=(n // t,),
        in_specs=[pl.BlockSpec((t, D_MODEL), lambda i: (i, 0)), _const_spec((1, D_MODEL))],
        out_specs=pl.BlockSpec((t, D_MODEL), lambda i: (i, 0)),
        out_shape=jax.ShapeDtypeStruct((n, D_MODEL), F32),
        compiler_params=_params("arbitrary"),
        name="final_norm",
    )(x, g)


def _t5_bucket_np(rel):
    nb = N_BUCKETS // 2
    max_exact = nb // 2
    n = np.abs(rel)
    large = max_exact + (np.log(np.maximum(n, 1).astype(np.float64) / max_exact)
                         / math.log(MAX_DISTANCE / max_exact) * (nb - max_exact)).astype(np.int32)
    large = np.minimum(large, nb - 1)
    return (np.where(rel > 0, nb, 0) + np.where(n < max_exact, n, large)).astype(np.int32)


def _band_buckets(rows, band, dil):
    rel = np.arange(rows + 2 * band, dtype=np.int32)[None, :] - band - np.arange(rows, dtype=np.int32)[:, None]
    return jnp.asarray(_t5_bucket_np(rel * dil))


def _position_tables(seq_len):
    t = jnp.arange(seq_len, dtype=jnp.int32)
    inv = ROPE_THETA ** (-jnp.arange(ROPE_HALF, dtype=F32) / ROPE_HALF)

    def cs(pos):
        ang = pos.astype(F32)[:, None] * inv[None, :]
        return jnp.cos(ang), jnp.sin(ang)

    def group(c, s):
        return jnp.concatenate([c, c], axis=1), jnp.concatenate([-s, s], axis=1)

    ones, zeros = jnp.ones((seq_len, 32), F32), jnp.zeros((seq_len, 32), F32)
    cb, sb = group(*cs(t))
    cr, sr = group(*cs(t // GRID_W))
    cc, sc = group(*cs(t % GRID_W))
    g64 = np.kron(np.eye(2, dtype=np.float32), np.full((HEAD_DIM, HEAD_DIM), 1.0 / HEAD_DIM, np.float32))
    return {
        "cos_b": jnp.concatenate([ones, ones, cb, ones], axis=1),
        "sin_b": jnp.concatenate([zeros, zeros, sb, zeros], axis=1),
        "cos_d": jnp.concatenate([cr, cc, cr, cc], axis=1),
        "sin_d": jnp.concatenate([sr, sc, sr, sc], axis=1),
        "g64": jnp.asarray(g64, BF16),
    }


def _pack_layer(l, p):
    w_in = p["w_in"][l]
    split = np.cumsum([0, 256, 128, 128, 256, 128, 32, 768, 768, 768, 256, 128, 128])
    qa, ka, va, cq, ckv, kr, qc, kc, vc, qd, kd, vd = [w_in[:, split[i]:split[i + 1]] for i in range(12)]

    def pair_heads(w):
        d = w.shape[0]
        return w.reshape(d, 4, HEAD_DIM)[:, jnp.array([0, 2, 1, 3])].reshape(d, 256)

    z = lambda w: jnp.zeros((w_in.shape[0], w), F32)
    packed = jnp.concatenate([pair_heads(qa), ka, va, cq, ckv, z(64), kr, z(32), qc, kc, vc, pair_heads(qd), kd, vd],
                             axis=1)
    uq = p["mla_w_uq"][l].reshape(MLA_Q_RANK, MLA_HEADS, MLA_NOPE + MLA_ROPE)
    uq = jnp.pad(uq, ((0, 0), (0, 0), (0, LANE - MLA_NOPE - MLA_ROPE))).reshape(MLA_Q_RANK, MLA_HEADS * LANE)
    ukv = p["mla_w_ukv"][l].reshape(MLA_KV_RANK, MLA_HEADS, MLA_NOPE + MLA_V)
    ukv_k = jnp.pad(ukv[:, :, :MLA_NOPE], ((0, 0), (0, 0), (0, LANE - MLA_NOPE))).reshape(MLA_KV_RANK, MLA_HEADS * LANE)
    ukv_v = ukv[:, :, MLA_NOPE:].reshape(MLA_KV_RANK, MLA_HEADS * MLA_V)
    wb = p["w_branch"][l]
    pair_rows = lambda w: w.reshape(4, HEAD_DIM, D_MODEL)[jnp.array([0, 2, 1, 3])].reshape(BRANCH_W, D_MODEL)
    wb = jnp.stack([pair_rows(wb[0]), wb[1], wb[2], pair_rows(wb[3])])
    wr = jnp.pad(p["w_router"][l], ((0, 0), (0, LANE - N_EXPERTS)))
    wr_hi = wr.astype(BF16)
    row = lambda g: g.reshape(1, -1).astype(F32)
    return {
        "g_mix": row(p["norm_mix"][l]), "w_in": packed.astype(BF16), "w_uq": uq.astype(BF16),
        "w_ukv_k": ukv_k.astype(BF16), "w_ukv_v": ukv_v.astype(BF16),
        "g_q": row(p["mla_q_norm"][l]), "g_kv": row(p["mla_kv_norm"][l]),
        "g_dq": row(jnp.tile(p["d_q_norm"][l], 2)), "g_dk": row(jnp.tile(p["d_k_norm"][l], 2)),
        "a_sink": p["a_sink"][l].astype(F32),
        "w_gate": p["w_gate"][l].astype(BF16), "w_branch": wb.astype(BF16), "w_out": p["w_out"][l].astype(BF16),
        "g_ffn": row(p["norm_ffn"][l]), "w_router_hi": wr_hi, "w_router_lo": (wr - wr_hi.astype(F32)).astype(BF16),
        "w_exp_gate": p["w_exp_gate"][l].astype(BF16), "w_exp_up": p["w_exp_up"][l].astype(BF16),
        "w_exp_down": p["w_exp_down"][l].astype(BF16),
    }


def _to_streams(a, b, l, g, dil):
    a = a.reshape(b, l // dil, dil, 3, 256)[:, :, :, g]
    return a.transpose(0, 2, 1, 3).reshape(b * dil, l // dil, 256)


def _from_streams(a, b, l, dil):
    return a.reshape(b, dil, l // dil, 256).transpose(0, 2, 1, 3).reshape(b * l, 256)


def _layer(x, b, l, lw, tabs, t5_table):
    n = b * l
    h, qa, ka, va, qb, kb, vb, qc, kc, vc, qd, kd, vd = _proj(x, l, lw, tabs)
    seq = lambda a: a.reshape(b, l, a.shape[-1])
    flat = lambda a: a.reshape(n, a.shape[-1])

    oa = flat(_attn_a(seq(qa), seq(ka), seq(va), t5_table[:, :A_HEADS], lw["a_sink"], tabs["bkt_a"]))
    ob = flat(_flash_b(seq(qb), seq(kb), seq(vb)))
    od = flat(_flash_d(seq(qd), seq(kd), seq(vd)))

    oc, lse = [], []
    for g, (_, dil) in enumerate(C_GROUPS):
        table = t5_table[:, A_HEADS + g * C_HEADS:A_HEADS + (g + 1) * C_HEADS]
        if dil == 1:
            o_g, lse_g = _attn_c(seq(qc), seq(kc), seq(vc), g, table, tabs["bkt_c"][g])
            oc.append(flat(o_g))
            lse.append(flat(lse_g))
        else:
            o_g, lse_g = _attn_c(_to_streams(qc, b, l, g, dil), _to_streams(kc, b, l, g, dil),
                                 _to_streams(vc, b, l, g, dil), 0, table, tabs["bkt_c"][g])
            oc.append(_from_streams(o_g, b, l, dil))
            lse.append(_from_streams(lse_g, b, l, dil))

    x_mid, h2, aff = _merge(x, h, oa, ob, oc, lse, od, lw)

    cap = EC_CAPACITY_FACTOR * n // N_EXPERTS
    gate, idx = lax.top_k(aff[:, :N_EXPERTS].T, cap)
    ye = _experts(h2[idx], gate[..., None], lw)
    return x_mid.at[idx.reshape(-1)].add(ye.reshape(-1, D_MODEL))


def _trunk(x, layers, t5_table, norm_final):
    b, l, _ = x.shape
    tabs = _position_tables(l)
    tabs["bkt_a"] = _band_buckets(A_WINDOW, A_WINDOW, 1)
    tabs["bkt_c"] = [_band_buckets(2 * C_BAND, C_BAND, dil) for _, dil in C_GROUPS]
    x = x.reshape(b * l, D_MODEL)
    for lw in layers:
        x = _layer(x, b, l, lw, tabs, t5_table)
    return _final_norm(x, norm_final.reshape(1, -1)).reshape(b, l, D_MODEL)


def kernel(x_prompt, x_sample, t5_table, norm_mix, w_in, a_sink, mla_q_norm, mla_w_uq, mla_kv_norm, mla_w_ukv,
           d_q_norm, d_k_norm, w_gate, w_branch, w_out, norm_ffn, w_router, w_exp_gate, w_exp_up, w_exp_down,
           norm_final):
    p = dict(norm_mix=norm_mix, w_in=w_in, a_sink=a_sink, mla_q_norm=mla_q_norm, mla_w_uq=mla_w_uq,
             mla_kv_norm=mla_kv_norm, mla_w_ukv=mla_w_ukv, d_q_norm=d_q_norm, d_k_norm=d_k_norm, w_gate=w_gate,
             w_branch=w_branch, w_out=w_out, norm_ffn=norm_ffn, w_router=w_router, w_exp_gate=w_exp_gate,
             w_exp_up=w_exp_up, w_exp_down=w_exp_down)
    layers = [_pack_layer(l, p) for l in range(w_in.shape[0])]
    t5 = t5_table.astype(F32)
    return (_trunk(x_prompt, layers, t5, norm_final), _trunk(x_sample, layers, t5, norm_final))
```

```python
import functools
import math

import numpy as np
import jax
import jax.numpy as jnp
from jax import lax
from jax.experimental import pallas as pl
from jax.experimental.pallas import tpu as pltpu

F32 = jnp.float32
BF16 = jnp.bfloat16

D_MODEL = 1024
HEAD_DIM = 64
A_HEADS = 4
A_WINDOW = 128
MLA_HEADS = 4
MLA_Q_RANK = 256
MLA_KV_RANK = 128
MLA_NOPE = 64
MLA_ROPE = 32
MLA_V = 64
C_HEADS = 4
C_GROUPS = ((128, 1), (512, 4), (2048, 16))
C_BAND = 64
GRID_W = 64
N_BRANCHES = 4
BRANCH_W = 256
N_BUCKETS = 32
MAX_DISTANCE = 1024
N_EXPERTS = 16
EC_CAPACITY_FACTOR = 2
ROPE_THETA = 10000.0
ROPE_HALF = 16
EPS = 1e-6
NEG = -1e30

LANE = 128
VMEM_LIMIT = 56 * 1024 * 1024

SEG_QA, SEG_KA, SEG_VA = 0, 256, 384
SEG_CQ, SEG_CKV, SEG_KR = 512, 768, 896
SEG_QC, SEG_KC, SEG_VC = 1024, 1792, 2560
SEG_QD, SEG_KD, SEG_VD = 3328, 3584, 3712
IN_COLS_PACKED = 3968

LOG2E = math.log2(math.e)
A_SCALE = HEAD_DIM ** -0.5
B_SCALE = (MLA_NOPE + MLA_ROPE) ** -0.5

TILE_PROJ = 512
TILE_MERGE = 256
TILE_EXPERT = 512
FLASH_ROWS = 256
FLASH_KV = 512
FLASH_SUB = 32
FLASH_UNROLL = 8


def _dot(a, b):
    return jnp.dot(a, b, preferred_element_type=F32)


def _dot_nt(a, b):
    return lax.dot_general(a, b, (((1,), (1,)), ((), ())), preferred_element_type=F32)


def _params(*sem):
    return pltpu.CompilerParams(dimension_semantics=sem, vmem_limit_bytes=VMEM_LIMIT)


def _const_spec(shape):
    zeros = (0,) * len(shape)
    return pl.BlockSpec(shape, lambda *_: zeros)


def _smem_spec():
    return pl.BlockSpec(memory_space=pltpu.SMEM)


def _lane_iota(shape):
    return lax.broadcasted_iota(jnp.int32, shape, len(shape) - 1)


def _rope128(x, cos, sin_signed, first_half):
    rot = jnp.where(first_half, pltpu.roll(x, LANE - ROPE_HALF, 1), pltpu.roll(x, ROPE_HALF, 1))
    return x * cos + rot * sin_signed


def _rms(x, gain):
    return x * lax.rsqrt(jnp.mean(x * x, axis=-1, keepdims=True) + EPS) * gain


def _lane_blocks(width):
    return [slice(c * LANE, (c + 1) * LANE) for c in range(width // LANE)]


def _proj_kernel(x_ref, gmix_ref, win_ref, wuq_ref, wukvk_ref, wukvv_ref, gq_ref, gkv_ref, gdq_ref, gdk_ref,
                 g64_ref, oneb_ref, oned_ref, cosb_ref, sinb_ref, cosd_ref, sind_ref,
                 h_ref, qa_ref, ka_ref, va_ref, qb_ref, kb_ref, vb_ref,
                 qc0_ref, kc0_ref, vc0_ref, qc1_ref, kc1_ref, vc1_ref, qc2_ref, kc2_ref, vc2_ref,
                 qd_ref, kd_ref, vd_ref, stage_ref):
    h = _rms(x_ref[...], gmix_ref[...]).astype(BF16)
    h_ref[...] = h

    def seg(lo, width):
        return _dot(h, win_ref[:, lo:lo + width])

    first_half = jnp.bitwise_and(_lane_iota((1, LANE)), 2 * ROPE_HALF - 1) < ROPE_HALF

    qa_ref[...] = (seg(SEG_QA, 256) * A_SCALE).astype(BF16)
    ka_ref[...] = seg(SEG_KA, 128).astype(BF16)
    va_ref[...] = seg(SEG_VA, 128).astype(BF16)

    cosb, sinb = cosb_ref[...], sinb_ref[...]
    cqn = _rms(seg(SEG_CQ, MLA_Q_RANK), gq_ref[...]).astype(BF16)
    qm = _dot(cqn, wuq_ref[...])
    for blk in _lane_blocks(MLA_HEADS * LANE):
        qb_ref[:, blk] = (_rope128(qm[:, blk], cosb, sinb, first_half) * (B_SCALE * LOG2E)).astype(BF16)
    ckvn = _rms(seg(SEG_CKV, MLA_KV_RANK), gkv_ref[...]).astype(BF16)
    k_pe = _rope128(seg(SEG_KR, LANE), cosb, sinb, first_half)
    kn = _dot(ckvn, wukvk_ref[...])
    for blk in _lane_blocks(MLA_HEADS * LANE):
        kb_ref[:, blk] = (kn[:, blk] + k_pe).astype(BF16)
    vb_ref[...] = (_dot(ckvn, wukvv_ref[...]) + oneb_ref[...]).astype(BF16)

    t = x_ref.shape[0]
    c_out = ((qc0_ref, kc0_ref, vc0_ref), (qc1_ref, kc1_ref, vc1_ref), (qc2_ref, kc2_ref, vc2_ref))
    for a, (lo, scale) in enumerate(((SEG_QC, A_SCALE), (SEG_KC, None), (SEG_VC, None))):
        z = seg(lo, 768)
        if scale is not None:
            z = z * scale
        c_out[0][a][...] = z[:, :256].astype(BF16)
        for g in (1, 2):
            dil = C_GROUPS[g][1]
            for c, blk in enumerate(_lane_blocks(256)):
                slot = (a * 2 + (g - 1)) * 2 + c
                stage_ref[slot] = z[:, g * 256 + c * LANE:g * 256 + (c + 1) * LANE]
                for r in range(dil):
                    c_out[g][a][r, :, blk] = stage_ref[slot, pl.ds(r, t // dil, stride=dil), :].astype(BF16)

    cosd, sind = cosd_ref[...], sind_ref[...]
    g64 = g64_ref[...]

    def head_norm(z, gain):
        sq = z * z
        hi = sq.astype(BF16)
        lo = (sq - hi.astype(F32)).astype(BF16)
        ms = _dot(hi, g64) + _dot(lo, g64)
        return z * lax.rsqrt(ms + EPS) * gain

    zq = seg(SEG_QD, 256)
    for blk in _lane_blocks(256):
        y = head_norm(zq[:, blk], gdq_ref[...])
        qd_ref[:, blk] = (_rope128(y, cosd, sind, first_half) * (A_SCALE * LOG2E)).astype(BF16)
    kd_ref[...] = _rope128(head_norm(seg(SEG_KD, 128), gdk_ref[...]), cosd, sind, first_half).astype(BF16)
    vd_ref[...] = (seg(SEG_VD, 256) + oned_ref[...]).astype(BF16)


def _proj(x, batch, seq_len, lw, tabs):
    n = x.shape[0]
    t = TILE_PROJ
    per_seq = seq_len // t
    row = lambda w: pl.BlockSpec((t, w), lambda i: (i, 0))
    tab = pl.BlockSpec((t, LANE), lambda i: (i % per_seq, 0))
    flat = lambda w: (row(w), jax.ShapeDtypeStruct((n, w), BF16))

    def planes(dil):
        return (pl.BlockSpec((None, dil, t // dil, 256), lambda i: (i // per_seq, 0, i % per_seq, 0)),
                jax.ShapeDtypeStruct((batch, dil, seq_len // dil, 256), BF16))

    outs = ([flat(D_MODEL), flat(256), flat(128), flat(128), flat(512), flat(512), flat(512)]
            + [flat(256)] * 3 + [planes(C_GROUPS[1][1])] * 3 + [planes(C_GROUPS[2][1])] * 3
            + [flat(256), flat(128), flat(256)])
    return pl.pallas_call(
        _proj_kernel,
        grid=(n // t,),
        in_specs=[row(D_MODEL), _const_spec((1, D_MODEL)), _const_spec((D_MODEL, IN_COLS_PACKED)),
                  _const_spec((MLA_Q_RANK, 512)), _const_spec((MLA_KV_RANK, 512)), _const_spec((MLA_KV_RANK, 512)),
                  _const_spec((1, MLA_Q_RANK)), _const_spec((1, MLA_KV_RANK)), _const_spec((1, LANE)),
                  _const_spec((1, LANE)), _const_spec((LANE, LANE)), _const_spec((1, 512)), _const_spec((1, 256)),
                  tab, tab, tab, tab],
        out_specs=[o[0] for o in outs],
        out_shape=[o[1] for o in outs],
        scratch_shapes=[pltpu.VMEM((12, t, LANE), F32)],
        compiler_params=_params("arbitrary"),
        name="norm_in_proj",
    )(x, lw["g_mix"], lw["w_in"], lw["w_uq"], lw["w_ukv_k"], lw["w_ukv_v"], lw["g_q"], lw["g_kv"], lw["g_dq"],
      lw["g_dk"], tabs["g64"], tabs["ones_b"], tabs["ones_d"], tabs["cos_b"], tabs["sin_b"], tabs["cos_d"],
      tabs["sin_d"])


def _build_bias(tab_ref, bkt_ref, bias_ref, n_heads, band):
    bkt = bkt_ref[...]
    row = lax.broadcasted_iota(jnp.int32, bkt.shape, 0)
    col = lax.broadcasted_iota(jnp.int32, bkt.shape, 1)
    valid = jnp.abs(col - band - row) <= band
    for hh in range(n_heads):
        def body(bb, acc, hh=hh):
            return jnp.where(bkt == bb, tab_ref[bb, hh], acc)
        acc = lax.fori_loop(0, N_BUCKETS, body, jnp.zeros(bkt.shape, F32))
        bias_ref[hh] = jnp.where(valid, acc, NEG)


def _attn_a_kernel(tab_ref, sink_ref, bkt_ref, q_ref, kp_ref, kc_ref, kn_ref, vp_ref, vc_ref, vn_ref,
                   o_ref, bias_ref):
    n = pl.program_id(1)
    last = pl.num_programs(1) - 1

    @pl.when((pl.program_id(0) == 0) & (n == 0))
    def _():
        _build_bias(tab_ref, bkt_ref, bias_ref, A_HEADS, A_WINDOW)

    w = A_WINDOW
    kcat = jnp.concatenate([kp_ref[...], kc_ref[...], kn_ref[...]], axis=0)
    vcat = jnp.concatenate([vp_ref[...], vc_ref[...], vn_ref[...]], axis=0)
    col = _lane_iota((w, 3 * w))
    edge = (col >= jnp.where(n > 0, 0, w)) & (col < jnp.where(n < last, 3 * w, 2 * w))
    is_lo = _lane_iota((1, LANE)) < HEAD_DIM
    q = q_ref[...]
    zero = jnp.zeros((w, LANE), BF16)
    for blk in range(2):
        qb = q[:, blk * LANE:(blk + 1) * LANE]
        res = []
        for half in range(2):
            head = blk + 2 * half
            qm = jnp.where(is_lo if half == 0 else jnp.logical_not(is_lo), qb, zero)
            s = jnp.where(edge, _dot_nt(qm, kcat) + bias_ref[head], NEG)
            sk = sink_ref[head]
            m = jnp.maximum(jnp.max(s, axis=-1, keepdims=True), sk)
            p = jnp.exp(s - m)
            denom = jnp.sum(p, axis=-1, keepdims=True) + jnp.exp(sk - m)
            res.append(_dot(p.astype(BF16), vcat) / denom)
        o_ref[:, blk * LANE:(blk + 1) * LANE] = jnp.where(is_lo, res[0], res[1]).astype(BF16)


def _attn_a(q, k, v, table, sink, bkt):
    b, l, _ = q.shape
    w = A_WINDOW
    nb = l // w
    kv = lambda f: pl.BlockSpec((None, w, LANE), lambda bi, ni: (bi, f(ni), 0))
    prev, cur, nxt = (lambda ni: jnp.maximum(ni - 1, 0)), (lambda ni: ni), (lambda ni: jnp.minimum(ni + 1, nb - 1))
    return pl.pallas_call(
        _attn_a_kernel,
        grid=(b, nb),
        in_specs=[_smem_spec(), _smem_spec(), _const_spec((w, 3 * w)),
                  pl.BlockSpec((None, w, 256), lambda bi, ni: (bi, ni, 0)),
                  kv(prev), kv(cur), kv(nxt), kv(prev), kv(cur), kv(nxt)],
        out_specs=pl.BlockSpec((None, w, 256), lambda bi, ni: (bi, ni, 0)),
        out_shape=jax.ShapeDtypeStruct((b, l, 256), BF16),
        scratch_shapes=[pltpu.VMEM((A_HEADS, w, 3 * w), F32)],
        compiler_params=_params("arbitrary", "arbitrary"),
        name="attn_window_sink",
    )(table, sink, bkt, q, k, k, k, v, v, v)


def _attn_c_kernel(tab_ref, bkt_ref, q_ref, kl_ref, kc_ref, kr_ref, vl_ref, vc_ref, vr_ref,
                   o_ref, lse_ref, bias_ref):
    n = pl.program_id(1)
    last = pl.num_programs(1) - 1

    @pl.when((pl.program_id(0) == 0) & (n == 0))
    def _():
        _build_bias(tab_ref, bkt_ref, bias_ref, C_HEADS, C_BAND)

    t = 2 * C_BAND
    col = _lane_iota((t, 2 * t))
    edge = (col >= jnp.where(n > 0, 0, C_BAND)) & (col < jnp.where(n < last, 2 * t, t + C_BAND))
    is_lo = _lane_iota((1, LANE)) < HEAD_DIM
    q = q_ref[...]
    zero = jnp.zeros((t, LANE), BF16)
    for blk in range(2):
        lanes = slice(blk * LANE, (blk + 1) * LANE)
        qb = q[:, lanes]
        kwin = jnp.concatenate([kl_ref[:, lanes], kc_ref[:, lanes], kr_ref[:, lanes]], axis=0)
        vwin = jnp.concatenate([vl_ref[:, lanes], vc_ref[:, lanes], vr_ref[:, lanes]], axis=0)
        outs, lses = [], []
        for half in range(2):
            head = 2 * blk + half
            qm = jnp.where(is_lo if half == 0 else jnp.logical_not(is_lo), qb, zero)
            s = jnp.where(edge, _dot_nt(qm, kwin) + bias_ref[head], NEG)
            m = jnp.max(s, axis=-1, keepdims=True)
            p = jnp.exp(s - m)
            denom = jnp.sum(p, axis=-1, keepdims=True)
            outs.append(_dot(p.astype(BF16), vwin) / denom)
            lses.append(m + jnp.log(denom))
        o_ref[:, lanes] = jnp.where(is_lo, outs[0], outs[1]).astype(BF16)
        lse_ref[:, lanes] = jnp.where(is_lo, lses[0], lses[1])


def _attn_c(q, k, v, table, bkt):
    s, m, _ = q.shape
    t = 2 * C_BAND
    nb = m // t
    nh = m // C_BAND
    halo = lambda f: pl.BlockSpec((None, C_BAND, 256), lambda si, ni: (si, f(ni), 0))
    left, right = (lambda ni: jnp.maximum(2 * ni - 1, 0)), (lambda ni: jnp.minimum(2 * ni + 2, nh - 1))
    mid = pl.BlockSpec((None, t, 256), lambda si, ni: (si, ni, 0))
    return pl.pallas_call(
        _attn_c_kernel,
        grid=(s, nb),
        in_specs=[_smem_spec(), _const_spec((t, 2 * t)), mid, halo(left), mid, halo(right), halo(left), mid,
                  halo(right)],
        out_specs=[mid, mid],
        out_shape=[jax.ShapeDtypeStruct((s, m, 256), BF16), jax.ShapeDtypeStruct((s, m, 256), F32)],
        scratch_shapes=[pltpu.VMEM((C_HEADS, t, 2 * t), F32)],
        compiler_params=_params("arbitrary", "arbitrary"),
        name="attn_dilated",
    )(table, bkt, q, k, k, k, v, v, v)


def _flash_chains(qs, k_refs, v_refs, s_ref, p_ref, m_ref, a_ref, acc_ref):
    nc = len(qs)
    rows = qs[0].shape[0]
    n_chunks = k_refs[0].shape[0] // FLASH_KV
    unroll = min(FLASH_UNROLL, n_chunks)
    ncol = FLASH_KV // LANE

    def scores(slot, chunk):
        start = pl.multiple_of(chunk * FLASH_KV, FLASH_KV)
        for c in range(nc):
            s_ref[slot, c] = _dot_nt(qs[c], k_refs[c][pl.ds(start, FLASH_KV), :])

    def consume(slot, chunk):
        start = pl.multiple_of(chunk * FLASH_KV, FLASH_KV)
        for c in range(nc):
            for r0 in range(0, rows, FLASH_SUB):
                rs = slice(r0, r0 + FLASH_SUB)
                sb = [s_ref[slot, c, rs, j * LANE:(j + 1) * LANE] for j in range(ncol)]
                smax = functools.reduce(jnp.maximum, sb)
                m_old = m_ref[c, rs, :]
                m_new = jnp.maximum(m_old, jnp.max(smax, axis=-1, keepdims=True))
                m_ref[c, rs, :] = m_new
                a_ref[c, rs, :] = jnp.exp2(m_old - m_new)
                for j in range(ncol):
                    p_ref[c, rs, j * LANE:(j + 1) * LANE] = jnp.exp2(sb[j] - m_new).astype(BF16)
            acc_ref[c] = a_ref[c] * acc_ref[c] + _dot(p_ref[c], v_refs[c][pl.ds(start, FLASH_KV), :])

    m_ref[...] = jnp.full(m_ref.shape, NEG, F32)
    acc_ref[...] = jnp.zeros(acc_ref.shape, F32)
    scores(0, 0)

    def trip(i, carry):
        base = i * unroll
        for u in range(unroll):
            scores((u + 1) % 2, base + u + 1)
            consume(u % 2, base + u)
        return carry

    lax.fori_loop(0, n_chunks // unroll - 1, trip, 0)
    base = n_chunks - unroll
    for u in range(unroll):
        if u + 1 < unroll:
            scores((u + 1) % 2, base + u + 1)
        consume(u % 2, base + u)
    return [acc_ref[c] / pltpu.roll(acc_ref[c], HEAD_DIM, 1) for c in range(nc)]


def _flash_scratch(nc, rows):
    stat = pltpu.VMEM((nc, rows, LANE), F32)
    return [pltpu.VMEM((2, nc, rows, FLASH_KV), F32), pltpu.VMEM((nc, rows, FLASH_KV), BF16), stat, stat, stat]


def _flash_b_kernel(qlo_ref, qhi_ref, klo_ref, khi_ref, vlo_ref, vhi_ref, o_ref, *scratch):
    o_lo, o_hi = _flash_chains([qlo_ref[...], qhi_ref[...]], [klo_ref, khi_ref], [vlo_ref, vhi_ref], *scratch)
    is_lo = _lane_iota((1, LANE)) < HEAD_DIM
    o_ref[...] = jnp.where(is_lo, o_lo, o_hi).astype(BF16)


def _flash_b(q, k, v):
    b, l, _ = q.shape
    tq = FLASH_ROWS
    qs = lambda off: pl.BlockSpec((None, tq, LANE), lambda bi, pi, qi: (bi, qi, 2 * pi + off))
    ks = lambda off: pl.BlockSpec((None, l, LANE), lambda bi, pi, qi: (bi, 0, 2 * pi + off))
    return pl.pallas_call(
        _flash_b_kernel,
        grid=(b, 2, l // tq),
        in_specs=[qs(0), qs(1), ks(0), ks(1), ks(0), ks(1)],
        out_specs=pl.BlockSpec((None, tq, LANE), lambda bi, pi, qi: (bi, qi, pi)),
        out_shape=jax.ShapeDtypeStruct((b, l, 256), BF16),
        scratch_shapes=_flash_scratch(2, FLASH_ROWS),
        compiler_params=_params("arbitrary", "arbitrary", "arbitrary"),
        name="attn_dense_mla",
    )(q, q, k, k, v, v)


def _flash_d_kernel(q_ref, k_ref, vlo_ref, vhi_ref, o_ref, *scratch):
    q = q_ref[...]
    tq = q.shape[0]
    is_lo = _lane_iota((1, LANE)) < HEAD_DIM
    zero = jnp.zeros((tq, LANE), BF16)
    qb = [q[:, :LANE], q[:, LANE:]]
    qs = [jnp.concatenate([jnp.where(keep, qb[0], zero), jnp.where(keep, qb[1], zero)], axis=0)
          for keep in (is_lo, jnp.logical_not(is_lo))]
    o_lo, o_hi = _flash_chains(qs, [k_ref, k_ref], [vlo_ref, vhi_ref], *scratch)
    o_ref[:, :LANE] = jnp.where(is_lo, o_lo[:tq], o_hi[:tq]).astype(BF16)
    o_ref[:, LANE:] = jnp.where(is_lo, o_lo[tq:], o_hi[tq:]).astype(BF16)


def _flash_d(q, k, v):
    b, l, _ = q.shape
    tq = FLASH_ROWS // 2
    whole = lambda blk: pl.BlockSpec((None, l, LANE), lambda bi, qi: (bi, 0, blk))
    tile = pl.BlockSpec((None, tq, 256), lambda bi, qi: (bi, qi, 0))
    return pl.pallas_call(
        _flash_d_kernel,
        grid=(b, l // tq),
        in_specs=[tile, whole(0), whole(0), whole(1)],
        out_specs=tile,
        out_shape=jax.ShapeDtypeStruct((b, l, 256), BF16),
        scratch_shapes=_flash_scratch(2, FLASH_ROWS),
        compiler_params=_params("arbitrary", "arbitrary"),
        name="attn_dense_axial",
    )(q, k, v, v)


def _merge_kernel(x_ref, h_ref, oa_ref, ob_ref, oc0_ref, l0_ref, oc1_ref, l1_ref, oc2_ref, l2_ref, od_ref,
                  wg_ref, wb_ref, wo_ref, gffn_ref, wrh_ref, wrl_ref, xo_ref, h2_ref, aff_ref, stage_ref):
    h = h_ref[...]
    t = h.shape[0]

    def token_order(ref, dil, slot):
        cols = []
        for c, blk in enumerate(_lane_blocks(256)):
            for r in range(dil):
                stage_ref[slot + c, pl.ds(r, t // dil, stride=dil), :] = ref[r, :, blk].astype(F32)
            cols.append(stage_ref[slot + c])
        return jnp.concatenate(cols, axis=1)

    o = [oc0_ref[...].astype(F32), token_order(oc1_ref, C_GROUPS[1][1], 0), token_order(oc2_ref, C_GROUPS[2][1], 2)]
    lse = [l0_ref[...], token_order(l1_ref, C_GROUPS[1][1], 4), token_order(l2_ref, C_GROUPS[2][1], 6)]
    mx = jnp.maximum(jnp.maximum(lse[0], lse[1]), lse[2])
    e = [jnp.exp(l - mx) for l in lse]
    den = e[0] + e[1] + e[2]
    oc = ((e[0] / den) * o[0] + (e[1] / den) * o[1] + (e[2] / den) * o[2]).astype(BF16)
    branches = (oa_ref[...], ob_ref[...], oc, od_ref[...])
    merged = None
    for bi in range(N_BRANCHES):
        term = jax.nn.sigmoid(_dot(h, wg_ref[bi])) * _dot(branches[bi], wb_ref[bi])
        merged = term if merged is None else merged + term
    xn = x_ref[...] + _dot(merged.astype(BF16), wo_ref[...])
    xo_ref[...] = xn
    hn = _rms(xn, gffn_ref[...])
    hi = hn.astype(BF16)
    h2_ref[...] = hi
    lo = (hn - hi.astype(F32)).astype(BF16)
    logits = _dot(hi, wrh_ref[...]) + _dot(lo, wrh_ref[...]) + _dot(hi, wrl_ref[...])
    logits = jnp.where(_lane_iota((1, LANE)) < N_EXPERTS, logits, NEG)
    ex = jnp.exp(logits - jnp.max(logits, axis=-1, keepdims=True))
    aff_ref[...] = ex / jnp.sum(ex, axis=-1, keepdims=True)


def _merge(x, h, oa, ob, oc, lse, od, seq_len, lw):
    n = x.shape[0]
    t = TILE_MERGE
    per_seq = seq_len // t
    row = lambda w: pl.BlockSpec((t, w), lambda i: (i, 0))
    planes = lambda dil: pl.BlockSpec((None, dil, t // dil, 256), lambda i: (i // per_seq, 0, i % per_seq, 0))
    p1, p2 = planes(C_GROUPS[1][1]), planes(C_GROUPS[2][1])
    return pl.pallas_call(
        _merge_kernel,
        grid=(n // t,),
        in_specs=[row(D_MODEL), row(D_MODEL), row(BRANCH_W), row(BRANCH_W), row(BRANCH_W), row(BRANCH_W),
                  p1, p1, p2, p2, row(BRANCH_W),
                  _const_spec((N_BRANCHES, D_MODEL, D_MODEL)), _const_spec((N_BRANCHES, BRANCH_W, D_MODEL)),
                  _const_spec((D_MODEL, D_MODEL)), _const_spec((1, D_MODEL)), _const_spec((D_MODEL, LANE)),
                  _const_spec((D_MODEL, LANE))],
        out_specs=[row(D_MODEL), row(D_MODEL), row(LANE)],
        out_shape=[jax.ShapeDtypeStruct((n, D_MODEL), F32), jax.ShapeDtypeStruct((n, D_MODEL), BF16),
                   jax.ShapeDtypeStruct((n, LANE), F32)],
        scratch_shapes=[pltpu.VMEM((8, t, LANE), F32)],
        compiler_params=_params("arbitrary"),
        name="gated_merge_router",
    )(x, h, oa, ob, oc[0], lse[0], oc[1], lse[1], oc[2], lse[2], od,
      lw["w_gate"], lw["w_branch"], lw["w_out"], lw["g_ffn"], lw["w_router_hi"], lw["w_router_lo"])


def _expert_kernel(xe_ref, gate_ref, wg_ref, wu_ref, wd_ref, ye_ref):
    xe = xe_ref[...]
    he = (jax.nn.silu(_dot(xe, wg_ref[...])) * _dot(xe, wu_ref[...])).astype(BF16)
    ye_ref[...] = _dot(he, wd_ref[...]) * gate_ref[...]


def _experts(xe, gate, lw):
    e, cap, d = xe.shape
    t = min(TILE_EXPERT, cap)
    ff = lw["w_exp_gate"].shape[-1]
    wspec = lambda a, b: pl.BlockSpec((None, a, b), lambda ei, ti: (ei, 0, 0))
    return pl.pallas_call(
        _expert_kernel,
        grid=(e, cap // t),
        in_specs=[pl.BlockSpec((None, t, d), lambda ei, ti: (ei, ti, 0)),
                  pl.BlockSpec((None, t, 1), lambda ei, ti: (ei, ti, 0)),
                  wspec(d, ff), wspec(d, ff), wspec(ff, d)],
        out_specs=pl.BlockSpec((None, t, d), lambda ei, ti: (ei, ti, 0)),
        out_shape=jax.ShapeDtypeStruct((e, cap, d), F32),
        compiler_params=_params("arbitrary", "arbitrary"),
        name="expert_swiglu",
    )(xe, gate, lw["w_exp_gate"], lw["w_exp_up"], lw["w_exp_down"])


def _final_kernel(x_ref, g_ref, o_ref):
    o_ref[...] = _rms(x_ref[...], g_ref[...])


def _final_norm(x, g):
    n = x.shape[0]
    t = TILE_PROJ
    return pl.pallas_call(
        _final_kernel,
        grid=(n // t,),
        in_specs=[pl.BlockSpec((t, D_MODEL), lambda i: (i, 0)), _const_spec((1, D_MODEL))],
        out_specs=pl.BlockSpec((t, D_MODEL), lambda i: (i, 0)),
        out_shape=jax.ShapeDtypeStruct((n, D_MODEL), F32),
        compiler_params=_params("arbitrary"),
        name="final_norm",
    )(x, g)


def _t5_bucket_np(rel):
    nb = N_BUCKETS // 2
    max_exact = nb // 2
    n = np.abs(rel)
    large = max_exact + (np.log(np.maximum(n, 1).astype(np.float64) / max_exact)
                         / math.log(MAX_DISTANCE / max_exact) * (nb - max_exact)).astype(np.int32)
    large = np.minimum(large, nb - 1)
    return (np.where(rel > 0, nb, 0) + np.where(n < max_exact, n, large)).astype(np.int32)


def _band_buckets(rows, band, dil):
    rel = np.arange(rows + 2 * band, dtype=np.int32)[None, :] - band - np.arange(rows, dtype=np.int32)[:, None]
    return jnp.asarray(_t5_bucket_np(rel * dil))


def _position_tables(seq_len):
    t = jnp.arange(seq_len, dtype=jnp.int32)
    inv = ROPE_THETA ** (-jnp.arange(ROPE_HALF, dtype=F32) / ROPE_HALF)

    def cs(pos):
        ang = pos.astype(F32)[:, None] * inv[None, :]
        return jnp.cos(ang), jnp.sin(ang)

    def group(c, s):
        return jnp.concatenate([c, c], axis=1), jnp.concatenate([-s, s], axis=1)

    ones, zeros = jnp.ones((seq_len, 32), F32), jnp.zeros((seq_len, 32), F32)
    cb, sb = group(*cs(t))
    cr, sr = group(*cs(t // GRID_W))
    cc, sc = group(*cs(t % GRID_W))
    g64 = np.kron(np.eye(2, dtype=np.float32), np.full((HEAD_DIM, HEAD_DIM), 1.0 / HEAD_DIM, np.float32))
    pad_ones = np.tile(np.concatenate([np.zeros(64), np.ones(128), np.zeros(64)]).astype(np.float32), 2)
    return {
        "cos_b": jnp.concatenate([ones, ones, cb, ones], axis=1),
        "sin_b": jnp.concatenate([zeros, zeros, sb, zeros], axis=1),
        "cos_d": jnp.concatenate([cr, cc, cr, cc], axis=1),
        "sin_d": jnp.concatenate([sr, sc, sr, sc], axis=1),
        "g64": jnp.asarray(g64, BF16),
        "ones_b": jnp.asarray(pad_ones.reshape(1, 512)),
        "ones_d": jnp.asarray(pad_ones[:256].reshape(1, 256)),
    }


def _pad_value_heads(w):
    d, heads, _ = w.shape
    z = jnp.zeros((d, heads // 2, HEAD_DIM), w.dtype)
    return jnp.stack([w[:, 0::2], z, z, w[:, 1::2]], axis=2).reshape(d, heads * LANE)


def _pack_layer(l, p):
    w_in = p["w_in"][l]
    split = np.cumsum([0, 256, 128, 128, 256, 128, 32, 768, 768, 768, 256, 128, 128])
    qa, ka, va, cq, ckv, kr, qc, kc, vc, qd, kd, vd = [w_in[:, split[i]:split[i + 1]] for i in range(12)]

    def pair_heads(w):
        d = w.shape[0]
        return w.reshape(d, 4, HEAD_DIM)[:, jnp.array([0, 2, 1, 3])].reshape(d, 256)

    z = lambda w: jnp.zeros((w_in.shape[0], w), F32)
    packed = jnp.concatenate([pair_heads(qa), ka, va, cq, ckv, z(64), kr, z(32), qc, kc, vc, pair_heads(qd), kd,
                              _pad_value_heads(vd.reshape(-1, 2, HEAD_DIM))], axis=1)
    uq = p["mla_w_uq"][l].reshape(MLA_Q_RANK, MLA_HEADS, MLA_NOPE + MLA_ROPE)
    uq = jnp.pad(uq, ((0, 0), (0, 0), (0, LANE - MLA_NOPE - MLA_ROPE))).reshape(MLA_Q_RANK, MLA_HEADS * LANE)
    ukv = p["mla_w_ukv"][l].reshape(MLA_KV_RANK, MLA_HEADS, MLA_NOPE + MLA_V)
    ukv_k = jnp.pad(ukv[:, :, :MLA_NOPE], ((0, 0), (0, 0), (0, LANE - MLA_NOPE))).reshape(MLA_KV_RANK, MLA_HEADS * LANE)
    ukv_v = _pad_value_heads(ukv[:, :, MLA_NOPE:])
    wb = p["w_branch"][l]
    pair_rows = lambda w: w.reshape(4, HEAD_DIM, D_MODEL)[jnp.array([0, 2, 1, 3])].reshape(BRANCH_W, D_MODEL)
    wb = jnp.stack([pair_rows(wb[0]), wb[1], wb[2], pair_rows(wb[3])])
    wr = jnp.pad(p["w_router"][l], ((0, 0), (0, LANE - N_EXPERTS)))
    wr_hi = wr.astype(BF16)
    row = lambda g: g.reshape(1, -1).astype(F32)
    return {
        "g_mix": row(p["norm_mix"][l]), "w_in": packed.astype(BF16), "w_uq": uq.astype(BF16),
        "w_ukv_k": ukv_k.astype(BF16), "w_ukv_v": ukv_v.astype(BF16),
        "g_q": row(p["mla_q_norm"][l]), "g_kv": row(p["mla_kv_norm"][l]),
        "g_dq": row(jnp.tile(p["d_q_norm"][l], 2)), "g_dk": row(jnp.tile(p["d_k_norm"][l], 2)),
        "a_sink": p["a_sink"][l].astype(F32),
        "w_gate": p["w_gate"][l].astype(BF16), "w_branch": wb.astype(BF16), "w_out": p["w_out"][l].astype(BF16),
        "g_ffn": row(p["norm_ffn"][l]), "w_router_hi": wr_hi, "w_router_lo": (wr - wr_hi.astype(F32)).astype(BF16),
        "w_exp_gate": p["w_exp_gate"][l].astype(BF16), "w_exp_up": p["w_exp_up"][l].astype(BF16),
        "w_exp_down": p["w_exp_down"][l].astype(BF16),
    }


def _layer(x, b, l, lw, tabs, t5_table):
    n = b * l
    (h, qa, ka, va, qb, kb, vb, qc0, kc0, vc0, qc1, kc1, vc1, qc2, kc2, vc2, qd, kd, vd) = _proj(x, b, l, lw, tabs)
    seq = lambda a: a.reshape(b, l, a.shape[-1])
    flat = lambda a: a.reshape(n, a.shape[-1])

    oa = flat(_attn_a(seq(qa), seq(ka), seq(va), t5_table[:, :A_HEADS], lw["a_sink"], tabs["bkt_a"]))
    ob = flat(_flash_b(seq(qb), seq(kb), seq(vb)))
    od = flat(_flash_d(seq(qd), seq(kd), seq(vd)))

    oc, lse = [], []
    for g, (q, k, v) in enumerate(((seq(qc0), seq(kc0), seq(vc0)), (qc1, kc1, vc1), (qc2, kc2, vc2))):
        dil = C_GROUPS[g][1]
        table = t5_table[:, A_HEADS + g * C_HEADS:A_HEADS + (g + 1) * C_HEADS]
        streams = lambda a: a.reshape(b * dil, l // dil, 256)
        o_g, lse_g = _attn_c(streams(q), streams(k), streams(v), table, tabs["bkt_c"][g])
        if dil == 1:
            oc.append(flat(o_g))
            lse.append(flat(lse_g))
        else:
            oc.append(o_g.reshape(b, dil, l // dil, 256))
            lse.append(lse_g.reshape(b, dil, l // dil, 256))

    x_mid, h2, aff = _merge(x, h, oa, ob, oc, lse, od, l, lw)

    cap = EC_CAPACITY_FACTOR * n // N_EXPERTS
    gate, idx = lax.top_k(aff[:, :N_EXPERTS].T, cap)
    ye = _experts(h2[idx], gate[..., None], lw)
    return x_mid.at[idx.reshape(-1)].add(ye.reshape(-1, D_MODEL))


def _trunk(x, layers, t5_table, norm_final):
    b, l, _ = x.shape
    tabs = _position_tables(l)
    tabs["bkt_a"] = _band_buckets(A_WINDOW, A_WINDOW, 1)
    tabs["bkt_c"] = [_band_buckets(2 * C_BAND, C_BAND, dil) for _, dil in C_GROUPS]
    x = x.reshape(b * l, D_MODEL)
    for lw in layers:
        x = _layer(x, b, l, lw, tabs, t5_table)
    return _final_norm(x, norm_final.reshape(1, -1)).reshape(b, l, D_MODEL)


def kernel(x_prompt, x_sample, t5_table, norm_mix, w_in, a_sink, mla_q_norm, mla_w_uq, mla_kv_norm, mla_w_ukv,
           d_q_norm, d_k_norm, w_gate, w_branch, w_out, norm_ffn, w_router, w_exp_gate, w_exp_up, w_exp_down,
           norm_final):
    p = dict(norm_mix=norm_mix, w_in=w_in, a_sink=a_sink, mla_q_norm=mla_q_norm, mla_w_uq=mla_w_uq,
             mla_kv_norm=mla_kv_norm, mla_w_ukv=mla_w_ukv, d_q_norm=d_q_norm, d_k_norm=d_k_norm, w_gate=w_gate,
             w_branch=w_branch, w_out=w_out, norm_ffn=norm_ffn, w_router=w_router, w_exp_gate=w_exp_gate,
             w_exp_up=w_exp_up, w_exp_down=w_exp_down)
    layers = [_pack_layer(l, p) for l in range(w_in.shape[0])]
    t5 = t5_table.astype(F32)
    return (_trunk(x_prompt, layers, t5, norm_final), _trunk(x_sample, layers, t5, norm_final))
```

```python
import functools
import math

import numpy as np
import jax
import jax.numpy as jnp
from jax import lax
from jax.experimental import pallas as pl
from jax.experimental.pallas import tpu as pltpu

F32 = jnp.float32
BF16 = jnp.bfloat16

D_MODEL = 1024
HEAD_DIM = 64
A_HEADS = 4
A_WINDOW = 128
MLA_HEADS = 4
MLA_Q_RANK = 256
MLA_KV_RANK = 128
MLA_NOPE = 64
MLA_ROPE = 32
MLA_V = 64
C_HEADS = 4
C_GROUPS = ((128, 1), (512, 4), (2048, 16))
C_BAND = 64
GRID_W = 64
N_BRANCHES = 4
BRANCH_W = 256
N_BUCKETS = 32
MAX_DISTANCE = 1024
N_EXPERTS = 16
EC_CAPACITY_FACTOR = 2
ROPE_THETA = 10000.0
ROPE_HALF = 16
EPS = 1e-6
NEG = -1e30

LANE = 128
VMEM_LIMIT = 56 * 1024 * 1024

SEG_QA, SEG_KA, SEG_VA = 0, 256, 384
SEG_CQ, SEG_CKV, SEG_KR = 512, 768, 896
SEG_QC, SEG_KC, SEG_VC = 1024, 1792, 2560
SEG_QD, SEG_KD, SEG_VD = 3328, 3584, 3712
IN_COLS_PACKED = 3968

LOG2E = math.log2(math.e)
A_SCALE = HEAD_DIM ** -0.5 * LOG2E
B_SCALE = (MLA_NOPE + MLA_ROPE) ** -0.5 * LOG2E

TILE_PROJ = 512
TILE_MERGE = 256
TILE_EXPERT = 512
BAND_TILE = 128
BAND_STEP = 512
FLASH_ROWS = 256
FLASH_KV = 512
FLASH_SUB = 32
FLASH_UNROLL = 8


def _dot(a, b):
    return jnp.dot(a, b, preferred_element_type=F32)


def _dot_nt(a, b):
    return lax.dot_general(a, b, (((1,), (1,)), ((), ())), preferred_element_type=F32)


def _params(*sem):
    return pltpu.CompilerParams(dimension_semantics=sem, vmem_limit_bytes=VMEM_LIMIT)


def _const_spec(shape):
    zeros = (0,) * len(shape)
    return pl.BlockSpec(shape, lambda *_: zeros, pipeline_mode=pl.Buffered(1))


def _smem_spec():
    return pl.BlockSpec(memory_space=pltpu.SMEM)


def _lane_iota(shape):
    return lax.broadcasted_iota(jnp.int32, shape, len(shape) - 1)


def _rope128(x, cos, sin_signed, first_half):
    rot = jnp.where(first_half, pltpu.roll(x, LANE - ROPE_HALF, 1), pltpu.roll(x, ROPE_HALF, 1))
    return x * cos + rot * sin_signed


def _rms(x, gain):
    return x * lax.rsqrt(jnp.mean(x * x, axis=-1, keepdims=True) + EPS) * gain


def _lane_blocks(width):
    return [slice(c * LANE, (c + 1) * LANE) for c in range(width // LANE)]


def _proj_kernel(x_ref, gmix_ref, win_ref, wuq_ref, wukvk_ref, wukvv_ref, gq_ref, gkv_ref, gdq_ref, gdk_ref,
                 g64_ref, oneb_ref, oned_ref, cosb_ref, sinb_ref, cosd_ref, sind_ref,
                 h_ref, qa_ref, ka_ref, va_ref, qb_ref, kb_ref, vb_ref,
                 qc0_ref, kc0_ref, vc0_ref, qc1_ref, kc1_ref, vc1_ref, qc2_ref, kc2_ref, vc2_ref,
                 qd_ref, kd_ref, vd_ref, stage_ref):
    h = _rms(x_ref[...], gmix_ref[...]).astype(BF16)
    h_ref[...] = h

    def seg(lo, width):
        return _dot(h, win_ref[:, lo:lo + width])

    first_half = jnp.bitwise_and(_lane_iota((1, LANE)), 2 * ROPE_HALF - 1) < ROPE_HALF

    qa_ref[...] = (seg(SEG_QA, 256) * A_SCALE).astype(BF16)
    ka_ref[...] = seg(SEG_KA, 128).astype(BF16)
    va_ref[...] = seg(SEG_VA, 128).astype(BF16)

    cosb, sinb = cosb_ref[...], sinb_ref[...]
    cqn = _rms(seg(SEG_CQ, MLA_Q_RANK), gq_ref[...]).astype(BF16)
    qm = _dot(cqn, wuq_ref[...])
    for blk in _lane_blocks(MLA_HEADS * LANE):
        qb_ref[:, blk] = (_rope128(qm[:, blk], cosb, sinb, first_half) * B_SCALE).astype(BF16)
    ckvn = _rms(seg(SEG_CKV, MLA_KV_RANK), gkv_ref[...]).astype(BF16)
    k_pe = _rope128(seg(SEG_KR, LANE), cosb, sinb, first_half)
    kn = _dot(ckvn, wukvk_ref[...])
    for blk in _lane_blocks(MLA_HEADS * LANE):
        kb_ref[:, blk] = (kn[:, blk] + k_pe).astype(BF16)
    vb_ref[...] = (_dot(ckvn, wukvv_ref[...]) + oneb_ref[...]).astype(BF16)

    t = x_ref.shape[0]
    c_out = ((qc0_ref, kc0_ref, vc0_ref), (qc1_ref, kc1_ref, vc1_ref), (qc2_ref, kc2_ref, vc2_ref))
    for a, (lo, scale) in enumerate(((SEG_QC, A_SCALE), (SEG_KC, None), (SEG_VC, None))):
        z = seg(lo, 768)
        if scale is not None:
            z = z * scale
        c_out[0][a][...] = z[:, :256].astype(BF16)
        for g in (1, 2):
            dil = C_GROUPS[g][1]
            for c, blk in enumerate(_lane_blocks(256)):
                slot = (a * 2 + (g - 1)) * 2 + c
                stage_ref[slot] = z[:, g * 256 + c * LANE:g * 256 + (c + 1) * LANE]
                for r in range(dil):
                    c_out[g][a][r, :, blk] = stage_ref[slot, pl.ds(r, t // dil, stride=dil), :].astype(BF16)

    cosd, sind = cosd_ref[...], sind_ref[...]
    g64 = g64_ref[...]

    def head_norm(z, gain):
        sq = z * z
        hi = sq.astype(BF16)
        lo = (sq - hi.astype(F32)).astype(BF16)
        ms = _dot(hi, g64) + _dot(lo, g64)
        return z * lax.rsqrt(ms + EPS) * gain

    zq = seg(SEG_QD, 256)
    for blk in _lane_blocks(256):
        y = head_norm(zq[:, blk], gdq_ref[...])
        qd_ref[:, blk] = (_rope128(y, cosd, sind, first_half) * A_SCALE).astype(BF16)
    kd_ref[...] = _rope128(head_norm(seg(SEG_KD, 128), gdk_ref[...]), cosd, sind, first_half).astype(BF16)
    vd_ref[...] = (seg(SEG_VD, 256) + oned_ref[...]).astype(BF16)


def _proj(x, batch, seq_len, lw, tabs):
    n = x.shape[0]
    t = TILE_PROJ
    per_seq = seq_len // t
    row = lambda w: pl.BlockSpec((t, w), lambda i: (i, 0))
    tab = pl.BlockSpec((t, LANE), lambda i: (i % per_seq, 0))
    flat = lambda w: (row(w), jax.ShapeDtypeStruct((n, w), BF16))

    def planes(dil):
        return (pl.BlockSpec((None, dil, t // dil, 256), lambda i: (i // per_seq, 0, i % per_seq, 0)),
                jax.ShapeDtypeStruct((batch, dil, seq_len // dil, 256), BF16))

    outs = ([flat(D_MODEL), flat(256), flat(128), flat(128), flat(512), flat(512), flat(512)]
            + [flat(256)] * 3 + [planes(C_GROUPS[1][1])] * 3 + [planes(C_GROUPS[2][1])] * 3
            + [flat(256), flat(128), flat(256)])
    return pl.pallas_call(
        _proj_kernel,
        grid=(n // t,),
        in_specs=[row(D_MODEL), _const_spec((1, D_MODEL)), _const_spec((D_MODEL, IN_COLS_PACKED)),
                  _const_spec((MLA_Q_RANK, 512)), _const_spec((MLA_KV_RANK, 512)), _const_spec((MLA_KV_RANK, 512)),
                  _const_spec((1, MLA_Q_RANK)), _const_spec((1, MLA_KV_RANK)), _const_spec((1, LANE)),
                  _const_spec((1, LANE)), _const_spec((LANE, LANE)), _const_spec((1, 512)), _const_spec((1, 256)),
                  tab, tab, tab, tab],
        out_specs=[o[0] for o in outs],
        out_shape=[o[1] for o in outs],
        scratch_shapes=[pltpu.VMEM((12, t, LANE), F32)],
        compiler_params=_params("arbitrary"),
        name="norm_in_proj",
    )(x, lw["g_mix"], lw["w_in"], lw["w_uq"], lw["w_ukv_k"], lw["w_ukv_v"], lw["g_q"], lw["g_kv"], lw["g_dq"],
      lw["g_dk"], tabs["g64"], tabs["ones_b"], tabs["ones_d"], tabs["cos_b"], tabs["sin_b"], tabs["cos_d"],
      tabs["sin_d"])


def _build_bias(tab_ref, bkt_ref, bias_ref, n_heads, band):
    bkt = bkt_ref[...]
    row = lax.broadcasted_iota(jnp.int32, bkt.shape, 0)
    col = lax.broadcasted_iota(jnp.int32, bkt.shape, 1)
    valid = jnp.abs(col - band - row) <= band
    for hh in range(n_heads):
        def body(bb, acc, hh=hh):
            return jnp.where(bkt == bb, tab_ref[bb, hh], acc)
        acc = lax.fori_loop(0, N_BUCKETS, body, jnp.zeros(bkt.shape, F32))
        bias_ref[hh] = jnp.where(valid, acc * LOG2E, NEG)


def _banded_kernel(*refs, band, n_heads, paired_heads, has_sink):
    refs = list(refs)
    tab_ref = refs.pop(0)
    sink_ref = refs.pop(0) if has_sink else None
    bkt_ref, q_ref, kl_ref, kc_ref, kr_ref, vl_ref, vc_ref, vr_ref, o_ref = refs[:9]
    lse_ref = None if has_sink else refs[9]
    bias_ref, s_ref = refs[-2:]
    n = pl.program_id(1)
    last = pl.num_programs(1) - 1

    @pl.when((pl.program_id(0) == 0) & (n == 0))
    def _():
        _build_bias(tab_ref, bkt_ref, bias_ref, n_heads, band)

    t = BAND_TILE
    win = t + 2 * band
    n_sub = q_ref.shape[0] // t
    col = _lane_iota((t, win))
    is_lo = _lane_iota((1, LANE)) < HEAD_DIM
    zero = jnp.zeros((t, LANE), BF16)
    for blk in range(2):
        lanes = slice(blk * LANE, (blk + 1) * LANE)
        kv_lanes = slice(0, LANE) if paired_heads else lanes
        kall = jnp.concatenate([kl_ref[:, kv_lanes], kc_ref[:, kv_lanes], kr_ref[:, kv_lanes]], axis=0)
        for j in range(n_sub):
            qb = q_ref[j * t:(j + 1) * t, lanes]
            for half in range(2):
                qm = jnp.where(is_lo if half == 0 else jnp.logical_not(is_lo), qb, zero)
                s_ref[(blk * n_sub + j) * 2 + half] = _dot_nt(qm, kall[j * t:j * t + win])
    for blk in range(2):
        lanes = slice(blk * LANE, (blk + 1) * LANE)
        kv_lanes = slice(0, LANE) if paired_heads else lanes
        vall = jnp.concatenate([vl_ref[:, kv_lanes], vc_ref[:, kv_lanes], vr_ref[:, kv_lanes]], axis=0)
        for j in range(n_sub):
            rows = slice(j * t, (j + 1) * t)
            vwin = vall[j * t:j * t + win]
            outs, lses = [], []
            for half in range(2):
                head = blk + 2 * half if paired_heads else 2 * blk + half
                s = s_ref[(blk * n_sub + j) * 2 + half] + bias_ref[head]
                if j == 0:
                    s = jnp.where(col >= jnp.where(n > 0, 0, band), s, NEG)
                if j == n_sub - 1:
                    s = jnp.where(col < jnp.where(n < last, win, t + band), s, NEG)
                m = jnp.max(s, axis=-1, keepdims=True)
                if has_sink:
                    sk = sink_ref[head] * LOG2E
                    m = jnp.maximum(m, sk)
                p = jnp.exp2(s - m)
                denom = jnp.sum(p, axis=-1, keepdims=True)
                if has_sink:
                    denom = denom + jnp.exp2(sk - m)
                outs.append(_dot(p.astype(BF16), vwin) / denom)
                if lse_ref is not None:
                    lses.append(m * (1.0 / LOG2E) + jnp.log(denom))
            o_ref[rows, lanes] = jnp.where(is_lo, outs[0], outs[1]).astype(BF16)
            if lse_ref is not None:
                lse_ref[rows, lanes] = jnp.where(is_lo, lses[0], lses[1])


def _banded(q, k, v, table, bkt, band, sink=None):
    s, m, _ = q.shape
    kv_w = k.shape[-1]
    rows = min(BAND_STEP, m)
    per = rows // band
    nh = m // band
    halo = lambda f: pl.BlockSpec((None, band, kv_w), lambda si, ni: (si, f(ni), 0))
    left, right = (lambda ni: jnp.maximum(per * ni - 1, 0)), (lambda ni: jnp.minimum(per * (ni + 1), nh - 1))
    mid = lambda w: pl.BlockSpec((None, rows, w), lambda si, ni: (si, ni, 0))
    win = BAND_TILE + 2 * band
    has_sink = sink is not None
    out_bf = jax.ShapeDtypeStruct((s, m, 256), BF16)
    return pl.pallas_call(
        functools.partial(_banded_kernel, band=band, n_heads=table.shape[1], paired_heads=has_sink,
                          has_sink=has_sink),
        grid=(s, m // rows),
        in_specs=[_smem_spec()] + ([_smem_spec()] if has_sink else [])
                 + [_const_spec((BAND_TILE, win)), mid(256), halo(left), mid(kv_w), halo(right), halo(left),
                    mid(kv_w), halo(right)],
        out_specs=mid(256) if has_sink else [mid(256), mid(256)],
        out_shape=out_bf if has_sink else [out_bf, jax.ShapeDtypeStruct((s, m, 256), F32)],
        scratch_shapes=[pltpu.VMEM((table.shape[1], BAND_TILE, win), F32),
                        pltpu.VMEM((4 * (rows // BAND_TILE), BAND_TILE, win), F32)],
        compiler_params=_params("arbitrary", "arbitrary"),
        name="attn_window_sink" if has_sink else "attn_dilated",
    )(*([table] + ([sink] if has_sink else []) + [bkt, q, k, k, k, v, v, v]))


def _flash_chains(qs, k_refs, v_refs, s_ref, p_ref, m_ref, a_ref, acc_ref):
    nc = len(qs)
    rows = qs[0].shape[0]
    n_chunks = k_refs[0].shape[0] // FLASH_KV
    unroll = min(FLASH_UNROLL, n_chunks)
    ncol = FLASH_KV // LANE

    def scores(slot, chunk):
        start = pl.multiple_of(chunk * FLASH_KV, FLASH_KV)
        for c in range(nc):
            s_ref[slot, c] = _dot_nt(qs[c], k_refs[c][pl.ds(start, FLASH_KV), :])

    def consume(slot, chunk):
        start = pl.multiple_of(chunk * FLASH_KV, FLASH_KV)
        for c in range(nc):
            for r0 in range(0, rows, FLASH_SUB):
                rs = slice(r0, r0 + FLASH_SUB)
                sb = [s_ref[slot, c, rs, j * LANE:(j + 1) * LANE] for j in range(ncol)]
                smax = functools.reduce(jnp.maximum, sb)
                m_old = m_ref[c, rs, :]
                m_new = jnp.maximum(m_old, jnp.max(smax, axis=-1, keepdims=True))
                m_ref[c, rs, :] = m_new
                a_ref[c, rs, :] = jnp.exp2(m_old - m_new)
                for j in range(ncol):
                    p_ref[c, rs, j * LANE:(j + 1) * LANE] = jnp.exp2(sb[j] - m_new).astype(BF16)
            acc_ref[c] = a_ref[c] * acc_ref[c] + _dot(p_ref[c], v_refs[c][pl.ds(start, FLASH_KV), :])

    m_ref[...] = jnp.full(m_ref.shape, NEG, F32)
    acc_ref[...] = jnp.zeros(acc_ref.shape, F32)
    scores(0, 0)

    def trip(i, carry):
        base = i * unroll
        for u in range(unroll):
            scores((u + 1) % 2, base + u + 1)
            consume(u % 2, base + u)
        return carry

    lax.fori_loop(0, n_chunks // unroll - 1, trip, 0)
    base = n_chunks - unroll
    for u in range(unroll):
        if u + 1 < unroll:
            scores((u + 1) % 2, base + u + 1)
        consume(u % 2, base + u)
    return [acc_ref[c] / pltpu.roll(acc_ref[c], HEAD_DIM, 1) for c in range(nc)]


def _flash_scratch(nc, rows):
    stat = pltpu.VMEM((nc, rows, LANE), F32)
    return [pltpu.VMEM((2, nc, rows, FLASH_KV), F32), pltpu.VMEM((nc, rows, FLASH_KV), BF16), stat, stat, stat]


def _flash_b_kernel(qlo_ref, qhi_ref, klo_ref, khi_ref, vlo_ref, vhi_ref, o_ref, *scratch):
    o_lo, o_hi = _flash_chains([qlo_ref[...], qhi_ref[...]], [klo_ref, khi_ref], [vlo_ref, vhi_ref], *scratch)
    is_lo = _lane_iota((1, LANE)) < HEAD_DIM
    o_ref[...] = jnp.where(is_lo, o_lo, o_hi).astype(BF16)


def _flash_b(q, k, v):
    b, l, _ = q.shape
    tq = FLASH_ROWS
    qs = lambda off: pl.BlockSpec((None, tq, LANE), lambda bi, pi, qi: (bi, qi, 2 * pi + off))
    ks = lambda off: pl.BlockSpec((None, l, LANE), lambda bi, pi, qi: (bi, 0, 2 * pi + off))
    return pl.pallas_call(
        _flash_b_kernel,
        grid=(b, 2, l // tq),
        in_specs=[qs(0), qs(1), ks(0), ks(1), ks(0), ks(1)],
        out_specs=pl.BlockSpec((None, tq, LANE), lambda bi, pi, qi: (bi, qi, pi)),
        out_shape=jax.ShapeDtypeStruct((b, l, 256), BF16),
        scratch_shapes=_flash_scratch(2, FLASH_ROWS),
        compiler_params=_params("arbitrary", "arbitrary", "arbitrary"),
        name="attn_dense_mla",
    )(q, q, k, k, v, v)


def _flash_d_kernel(q_ref, k_ref, vlo_ref, vhi_ref, o_ref, *scratch):
    q = q_ref[...]
    tq = q.shape[0]
    is_lo = _lane_iota((1, LANE)) < HEAD_DIM
    zero = jnp.zeros((tq, LANE), BF16)
    qb = [q[:, :LANE], q[:, LANE:]]
    qs = [jnp.concatenate([jnp.where(keep, qb[0], zero), jnp.where(keep, qb[1], zero)], axis=0)
          for keep in (is_lo, jnp.logical_not(is_lo))]
    o_lo, o_hi = _flash_chains(qs, [k_ref, k_ref], [vlo_ref, vhi_ref], *scratch)
    o_ref[:, :LANE] = jnp.where(is_lo, o_lo[:tq], o_hi[:tq]).astype(BF16)
    o_ref[:, LANE:] = jnp.where(is_lo, o_lo[tq:], o_hi[tq:]).astype(BF16)


def _flash_d(q, k, v):
    b, l, _ = q.shape
    tq = FLASH_ROWS // 2
    whole = lambda blk: pl.BlockSpec((None, l, LANE), lambda bi, qi: (bi, 0, blk))
    tile = pl.BlockSpec((None, tq, 256), lambda bi, qi: (bi, qi, 0))
    return pl.pallas_call(
        _flash_d_kernel,
        grid=(b, l // tq),
        in_specs=[tile, whole(0), whole(0), whole(1)],
        out_specs=tile,
        out_shape=jax.ShapeDtypeStruct((b, l, 256), BF16),
        scratch_shapes=_flash_scratch(2, FLASH_ROWS),
        compiler_params=_params("arbitrary", "arbitrary"),
        name="attn_dense_axial",
    )(q, k, v, v)


def _merge_kernel(x_ref, h_ref, oa_ref, ob_ref, oc0_ref, l0_ref, oc1_ref, l1_ref, oc2_ref, l2_ref, od_ref,
                  wg_ref, wb_ref, wo_ref, gffn_ref, wrh_ref, wrl_ref, xo_ref, h2_ref, aff_ref, stage_ref):
    h = h_ref[...]
    t = h.shape[0]

    def token_order(ref, dil, slot):
        cols = []
        for c, blk in enumerate(_lane_blocks(256)):
            for r in range(dil):
                stage_ref[slot + c, pl.ds(r, t // dil, stride=dil), :] = ref[r, :, blk].astype(F32)
            cols.append(stage_ref[slot + c])
        return jnp.concatenate(cols, axis=1)

    o = [oc0_ref[...].astype(F32), token_order(oc1_ref, C_GROUPS[1][1], 0), token_order(oc2_ref, C_GROUPS[2][1], 2)]
    lse = [l0_ref[...], token_order(l1_ref, C_GROUPS[1][1], 4), token_order(l2_ref, C_GROUPS[2][1], 6)]
    mx = jnp.maximum(jnp.maximum(lse[0], lse[1]), lse[2])
    e = [jnp.exp(l - mx) for l in lse]
    den = e[0] + e[1] + e[2]
    oc = ((e[0] / den) * o[0] + (e[1] / den) * o[1] + (e[2] / den) * o[2]).astype(BF16)
    branches = (oa_ref[...], ob_ref[...], oc, od_ref[...])
    merged = None
    for bi in range(N_BRANCHES):
        term = jax.nn.sigmoid(_dot(h, wg_ref[bi])) * _dot(branches[bi], wb_ref[bi])
        merged = term if merged is None else merged + term
    xn = x_ref[...] + _dot(merged.astype(BF16), wo_ref[...])
    xo_ref[...] = xn
    hn = _rms(xn, gffn_ref[...])
    hi = hn.astype(BF16)
    h2_ref[...] = hi
    lo = (hn - hi.astype(F32)).astype(BF16)
    logits = _dot(hi, wrh_ref[...]) + _dot(lo, wrh_ref[...]) + _dot(hi, wrl_ref[...])
    logits = jnp.where(_lane_iota((1, LANE)) < N_EXPERTS, logits, NEG)
    ex = jnp.exp(logits - jnp.max(logits, axis=-1, keepdims=True))
    aff_ref[...] = ex / jnp.sum(ex, axis=-1, keepdims=True)


def _merge(x, h, oa, ob, oc, lse, od, seq_len, lw):
    n = x.shape[0]
    t = TILE_MERGE
    per_seq = seq_len // t
    row = lambda w: pl.BlockSpec((t, w), lambda i: (i, 0))
    planes = lambda dil: pl.BlockSpec((None, dil, t // dil, 256), lambda i: (i // per_seq, 0, i % per_seq, 0))
    p1, p2 = planes(C_GROUPS[1][1]), planes(C_GROUPS[2][1])
    return pl.pallas_call(
        _merge_kernel,
        grid=(n // t,),
        in_specs=[row(D_MODEL), row(D_MODEL), row(BRANCH_W), row(BRANCH_W), row(BRANCH_W), row(BRANCH_W),
                  p1, p1, p2, p2, row(BRANCH_W),
                  _const_spec((N_BRANCHES, D_MODEL, D_MODEL)), _const_spec((N_BRANCHES, BRANCH_W, D_MODEL)),
                  _const_spec((D_MODEL, D_MODEL)), _const_spec((1, D_MODEL)), _const_spec((D_MODEL, LANE)),
                  _const_spec((D_MODEL, LANE))],
        out_specs=[row(D_MODEL), row(D_MODEL), row(LANE)],
        out_shape=[jax.ShapeDtypeStruct((n, D_MODEL), F32), jax.ShapeDtypeStruct((n, D_MODEL), BF16),
                   jax.ShapeDtypeStruct((n, LANE), F32)],
        scratch_shapes=[pltpu.VMEM((8, t, LANE), F32)],
        compiler_params=_params("arbitrary"),
        name="gated_merge_router",
    )(x, h, oa, ob, oc[0], lse[0], oc[1], lse[1], oc[2], lse[2], od,
      lw["w_gate"], lw["w_branch"], lw["w_out"], lw["g_ffn"], lw["w_router_hi"], lw["w_router_lo"])


def _expert_kernel(xe_ref, gate_ref, wg_ref, wu_ref, wd_ref, ye_ref):
    xe = xe_ref[...]
    he = (jax.nn.silu(_dot(xe, wg_ref[...])) * _dot(xe, wu_ref[...])).astype(BF16)
    ye_ref[...] = _dot(he, wd_ref[...]) * gate_ref[...]


def _experts(xe, gate, lw):
    e, cap, d = xe.shape
    t = min(TILE_EXPERT, cap)
    ff = lw["w_exp_gate"].shape[-1]
    wspec = lambda a, b: pl.BlockSpec((None, a, b), lambda ei, ti: (ei, 0, 0))
    return pl.pallas_call(
        _expert_kernel,
        grid=(e, cap // t),
        in_specs=[pl.BlockSpec((None, t, d), lambda ei, ti: (ei, ti, 0)),
                  pl.BlockSpec((None, t, 1), lambda ei, ti: (ei, ti, 0)),
                  wspec(d, ff), wspec(d, ff), wspec(ff, d)],
        out_specs=pl.BlockSpec((None, t, d), lambda ei, ti: (ei, ti, 0)),
        out_shape=jax.ShapeDtypeStruct((e, cap, d), F32),
        compiler_params=_params("arbitrary", "arbitrary"),
        name="expert_swiglu",
    )(xe, gate, lw["w_exp_gate"], lw["w_exp_up"], lw["w_exp_down"])


def _final_kernel(x_ref, g_ref, o_ref):
    o_ref[...] = _rms(x_ref[...], g_ref[...])


def _final_norm(x, g):
    n = x.shape[0]
    t = TILE_PROJ
    return pl.pallas_call(
        _final_kernel,
        grid=(n // t,),
        in_specs=[pl.BlockSpec((t, D_MODEL), lambda i: (i, 0)), _const_spec((1, D_MODEL))],
        out_specs=pl.BlockSpec((t, D_MODEL), lambda i: (i, 0)),
        out_shape=jax.ShapeDtypeStruct((n, D_MODEL), F32),
        compiler_params=_params("arbitrary"),
        name="final_norm",
    )(x, g)


def _t5_bucket_np(rel):
    nb = N_BUCKETS // 2
    max_exact = nb // 2
    n = np.abs(rel)
    large = max_exact + (np.log(np.maximum(n, 1).astype(np.float64) / max_exact)
                         / math.log(MAX_DISTANCE / max_exact) * (nb - max_exact)).astype(np.int32)
    large = np.minimum(large, nb - 1)
    return (np.where(rel > 0, nb, 0) + np.where(n < max_exact, n, large)).astype(np.int32)


def _band_buckets(rows, band, dil):
    rel = np.arange(rows + 2 * band, dtype=np.int32)[None, :] - band - np.arange(rows, dtype=np.int32)[:, None]
    return jnp.asarray(_t5_bucket_np(rel * dil))


def _position_tables(seq_len):
    t = jnp.arange(seq_len, dtype=jnp.int32)
    inv = ROPE_THETA ** (-jnp.arange(ROPE_HALF, dtype=F32) / ROPE_HALF)

    def cs(pos):
        ang = pos.astype(F32)[:, None] * inv[None, :]
        return jnp.cos(ang), jnp.sin(ang)

    def group(c, s):
        return jnp.concatenate([c, c], axis=1), jnp.concatenate([-s, s], axis=1)

    ones, zeros = jnp.ones((seq_len, 32), F32), jnp.zeros((seq_len, 32), F32)
    cb, sb = group(*cs(t))
    cr, sr = group(*cs(t // GRID_W))
    cc, sc = group(*cs(t % GRID_W))
    g64 = np.kron(np.eye(2, dtype=np.float32), np.full((HEAD_DIM, HEAD_DIM), 1.0 / HEAD_DIM, np.float32))
    pad_ones = np.tile(np.concatenate([np.zeros(64), np.ones(128), np.zeros(64)]).astype(np.float32), 2)
    return {
        "cos_b": jnp.concatenate([ones, ones, cb, ones], axis=1),
        "sin_b": jnp.concatenate([zeros, zeros, sb, zeros], axis=1),
        "cos_d": jnp.concatenate([cr, cc, cr, cc], axis=1),
        "sin_d": jnp.concatenate([sr, sc, sr, sc], axis=1),
        "g64": jnp.asarray(g64, BF16),
        "ones_b": jnp.asarray(pad_ones.reshape(1, 512)),
        "ones_d": jnp.asarray(pad_ones[:256].reshape(1, 256)),
    }


def _pad_value_heads(w):
    d, heads, _ = w.shape
    z = jnp.zeros((d, heads // 2, HEAD_DIM), w.dtype)
    return jnp.stack([w[:, 0::2], z, z, w[:, 1::2]], axis=2).reshape(d, heads * LANE)


def _pack_layer(l, p):
    w_in = p["w_in"][l]
    split = np.cumsum([0, 256, 128, 128, 256, 128, 32, 768, 768, 768, 256, 128, 128])
    qa, ka, va, cq, ckv, kr, qc, kc, vc, qd, kd, vd = [w_in[:, split[i]:split[i + 1]] for i in range(12)]

    def pair_heads(w):
        d = w.shape[0]
        return w.reshape(d, 4, HEAD_DIM)[:, jnp.array([0, 2, 1, 3])].reshape(d, 256)

    z = lambda w: jnp.zeros((w_in.shape[0], w), F32)
    packed = jnp.concatenate([pair_heads(qa), ka, va, cq, ckv, z(64), kr, z(32), qc, kc, vc, pair_heads(qd), kd,
                              _pad_value_heads(vd.reshape(-1, 2, HEAD_DIM))], axis=1)
    uq = p["mla_w_uq"][l].reshape(MLA_Q_RANK, MLA_HEADS, MLA_NOPE + MLA_ROPE)
    uq = jnp.pad(uq, ((0, 0), (0, 0), (0, LANE - MLA_NOPE - MLA_ROPE))).reshape(MLA_Q_RANK, MLA_HEADS * LANE)
    ukv = p["mla_w_ukv"][l].reshape(MLA_KV_RANK, MLA_HEADS, MLA_NOPE + MLA_V)
    ukv_k = jnp.pad(ukv[:, :, :MLA_NOPE], ((0, 0), (0, 0), (0, LANE - MLA_NOPE))).reshape(MLA_KV_RANK, MLA_HEADS * LANE)
    ukv_v = _pad_value_heads(ukv[:, :, MLA_NOPE:])
    wb = p["w_branch"][l]
    pair_rows = lambda w: w.reshape(4, HEAD_DIM, D_MODEL)[jnp.array([0, 2, 1, 3])].reshape(BRANCH_W, D_MODEL)
    wb = jnp.stack([pair_rows(wb[0]), wb[1], wb[2], pair_rows(wb[3])])
    wr = jnp.pad(p["w_router"][l], ((0, 0), (0, LANE - N_EXPERTS)))
    wr_hi = wr.astype(BF16)
    row = lambda g: g.reshape(1, -1).astype(F32)
    return {
        "g_mix": row(p["norm_mix"][l]), "w_in": packed.astype(BF16), "w_uq": uq.astype(BF16),
        "w_ukv_k": ukv_k.astype(BF16), "w_ukv_v": ukv_v.astype(BF16),
        "g_q": row(p["mla_q_norm"][l]), "g_kv": row(p["mla_kv_norm"][l]),
        "g_dq": row(jnp.tile(p["d_q_norm"][l], 2)), "g_dk": row(jnp.tile(p["d_k_norm"][l], 2)),
        "a_sink": p["a_sink"][l].astype(F32),
        "w_gate": p["w_gate"][l].astype(BF16), "w_branch": wb.astype(BF16), "w_out": p["w_out"][l].astype(BF16),
        "g_ffn": row(p["norm_ffn"][l]), "w_router_hi": wr_hi, "w_router_lo": (wr - wr_hi.astype(F32)).astype(BF16),
        "w_exp_gate": p["w_exp_gate"][l].astype(BF16), "w_exp_up": p["w_exp_up"][l].astype(BF16),
        "w_exp_down": p["w_exp_down"][l].astype(BF16),
    }


def _layer(x, b, l, lw, tabs, t5_table):
    n = b * l
    (h, qa, ka, va, qb, kb, vb, qc0, kc0, vc0, qc1, kc1, vc1, qc2, kc2, vc2, qd, kd, vd) = _proj(x, b, l, lw, tabs)
    seq = lambda a: a.reshape(b, l, a.shape[-1])
    flat = lambda a: a.reshape(n, a.shape[-1])

    oa = flat(_banded(seq(qa), seq(ka), seq(va), t5_table[:, :A_HEADS], tabs["bkt_a"], A_WINDOW, lw["a_sink"]))
    ob = flat(_flash_b(seq(qb), seq(kb), seq(vb)))
    od = flat(_flash_d(seq(qd), seq(kd), seq(vd)))

    oc, lse = [], []
    for g, (q, k, v) in enumerate(((seq(qc0), seq(kc0), seq(vc0)), (qc1, kc1, vc1), (qc2, kc2, vc2))):
        dil = C_GROUPS[g][1]
        table = t5_table[:, A_HEADS + g * C_HEADS:A_HEADS + (g + 1) * C_HEADS]
        streams = lambda a: a.reshape(b * dil, l // dil, 256)
        o_g, lse_g = _banded(streams(q), streams(k), streams(v), table, tabs["bkt_c"][g], C_BAND)
        if dil == 1:
            oc.append(flat(o_g))
            lse.append(flat(lse_g))
        else:
            oc.append(o_g.reshape(b, dil, l // dil, 256))
            lse.append(lse_g.reshape(b, dil, l // dil, 256))

    x_mid, h2, aff = _merge(x, h, oa, ob, oc, lse, od, l, lw)

    cap = EC_CAPACITY_FACTOR * n // N_EXPERTS
    gate, idx = lax.top_k(aff[:, :N_EXPERTS].T, cap)
    ye = _experts(h2[idx], gate[..., None], lw)
    return x_mid.at[idx.reshape(-1)].add(ye.reshape(-1, D_MODEL))


def _trunk(x, layers, t5_table, norm_final):
    b, l, _ = x.shape
    tabs = _position_tables(l)
    tabs["bkt_a"] = _band_buckets(A_WINDOW, A_WINDOW, 1)
    tabs["bkt_c"] = [_band_buckets(2 * C_BAND, C_BAND, dil) for _, dil in C_GROUPS]
    x = x.reshape(b * l, D_MODEL)
    for lw in layers:
        x = _layer(x, b, l, lw, tabs, t5_table)
    return _final_norm(x, norm_final.reshape(1, -1)).reshape(b, l, D_MODEL)


def kernel(x_prompt, x_sample, t5_table, norm_mix, w_in, a_sink, mla_q_norm, mla_w_uq, mla_kv_norm, mla_w_ukv,
           d_q_norm, d_k_norm, w_gate, w_branch, w_out, norm_ffn, w_router, w_exp_gate, w_exp_up, w_exp_down,
           norm_final):
    p = dict(norm_mix=norm_mix, w_in=w_in, a_sink=a_sink, mla_q_norm=mla_q_norm, mla_w_uq=mla_w_uq,
             mla_kv_norm=mla_kv_norm, mla_w_ukv=mla_w_ukv, d_q_norm=d_q_norm, d_k_norm=d_k_norm, w_gate=w_gate,
             w_branch=w_branch, w_out=w_out, norm_ffn=norm_ffn, w_router=w_router, w_exp_gate=w_exp_gate,
             w_exp_up=w_exp_up, w_exp_down=w_exp_down)
    layers = [_pack_layer(l, p) for l in range(w_in.shape[0])]
    t5 = t5_table.astype(F32)
    return (_trunk(x_prompt, layers, t5, norm_final), _trunk(x_sample, layers, t5, norm_final))
```

```python
import functools
import math

import numpy as np
import jax
import jax.numpy as jnp
from jax import lax
from jax.experimental import pallas as pl
from jax.experimental.pallas import tpu as pltpu

F32 = jnp.float32
BF16 = jnp.bfloat16

D_MODEL = 1024
HEAD_DIM = 64
A_HEADS = 4
A_WINDOW = 128
MLA_HEADS = 4
MLA_Q_RANK = 256
MLA_KV_RANK = 128
MLA_NOPE = 64
MLA_ROPE = 32
MLA_V = 64
C_HEADS = 4
C_GROUPS = ((128, 1), (512, 4), (2048, 16))
C_BAND = 64
GRID_W = 64
N_BRANCHES = 4
BRANCH_W = 256
N_BUCKETS = 32
MAX_DISTANCE = 1024
N_EXPERTS = 16
EC_CAPACITY_FACTOR = 2
ROPE_THETA = 10000.0
ROPE_HALF = 16
EPS = 1e-6
NEG = -1e30

LANE = 128
VMEM_LIMIT = 56 * 1024 * 1024

SEG_QA, SEG_KA, SEG_VA = 0, 256, 384
SEG_CQ, SEG_CKV, SEG_KR = 512, 768, 896
SEG_QC, SEG_KC, SEG_VC = 1024, 1792, 2560
SEG_QD, SEG_KD, SEG_VD = 3328, 3584, 3712
IN_COLS_PACKED = 3968

LOG2E = math.log2(math.e)
A_SCALE = HEAD_DIM ** -0.5 * LOG2E
B_SCALE = (MLA_NOPE + MLA_ROPE) ** -0.5 * LOG2E

TILE_PROJ = 512
TILE_MERGE = 256
TILE_EXPERT = 512
TILE_COMBINE = 256
COMBINE_WIN = 768
BAND_TILE = 128
BAND_STEP = 512
FLASH_ROWS = 256
FLASH_KV = 512
FLASH_SUB = 32
FLASH_UNROLL = 8


def _dot(a, b):
    return jnp.dot(a, b, preferred_element_type=F32)


def _dot_nt(a, b):
    return lax.dot_general(a, b, (((1,), (1,)), ((), ())), preferred_element_type=F32)


def _params(*sem):
    return pltpu.CompilerParams(dimension_semantics=sem, vmem_limit_bytes=VMEM_LIMIT)


def _const_spec(shape):
    zeros = (0,) * len(shape)
    return pl.BlockSpec(shape, lambda *_: zeros, pipeline_mode=pl.Buffered(1))


def _smem_spec():
    return pl.BlockSpec(memory_space=pltpu.SMEM)


def _lane_iota(shape):
    return lax.broadcasted_iota(jnp.int32, shape, len(shape) - 1)


def _rope128(x, cos, sin_signed, first_half):
    rot = jnp.where(first_half, pltpu.roll(x, LANE - ROPE_HALF, 1), pltpu.roll(x, ROPE_HALF, 1))
    return x * cos + rot * sin_signed


def _rms(x, gain):
    return x * lax.rsqrt(jnp.mean(x * x, axis=-1, keepdims=True) + EPS) * gain


def _lane_blocks(width):
    return [slice(c * LANE, (c + 1) * LANE) for c in range(width // LANE)]


def _proj_kernel(x_ref, gmix_ref, win_ref, wuq_ref, wukvk_ref, wukvv_ref, gq_ref, gkv_ref, gdq_ref, gdk_ref,
                 g64_ref, oneb_ref, oned_ref, cosb_ref, sinb_ref, cosd_ref, sind_ref,
                 h_ref, qa_ref, ka_ref, va_ref, qb_ref, kb_ref, vb_ref,
                 qc0_ref, kc0_ref, vc0_ref, qc1_ref, kc1_ref, vc1_ref, qc2_ref, kc2_ref, vc2_ref,
                 qd_ref, kd_ref, vd_ref, stage_ref):
    h = _rms(x_ref[...], gmix_ref[...]).astype(BF16)
    h_ref[...] = h

    def seg(lo, width):
        return _dot(h, win_ref[:, lo:lo + width])

    first_half = jnp.bitwise_and(_lane_iota((1, LANE)), 2 * ROPE_HALF - 1) < ROPE_HALF

    qa_ref[...] = (seg(SEG_QA, 256) * A_SCALE).astype(BF16)
    ka_ref[...] = seg(SEG_KA, 128).astype(BF16)
    va_ref[...] = seg(SEG_VA, 128).astype(BF16)

    cosb, sinb = cosb_ref[...], sinb_ref[...]
    cqn = _rms(seg(SEG_CQ, MLA_Q_RANK), gq_ref[...]).astype(BF16)
    qm = _dot(cqn, wuq_ref[...])
    for blk in _lane_blocks(MLA_HEADS * LANE):
        qb_ref[:, blk] = (_rope128(qm[:, blk], cosb, sinb, first_half) * B_SCALE).astype(BF16)
    ckvn = _rms(seg(SEG_CKV, MLA_KV_RANK), gkv_ref[...]).astype(BF16)
    k_pe = _rope128(seg(SEG_KR, LANE), cosb, sinb, first_half)
    kn = _dot(ckvn, wukvk_ref[...])
    for blk in _lane_blocks(MLA_HEADS * LANE):
        kb_ref[:, blk] = (kn[:, blk] + k_pe).astype(BF16)
    vb_ref[...] = (_dot(ckvn, wukvv_ref[...]) + oneb_ref[...]).astype(BF16)

    t = x_ref.shape[0]
    c_out = ((qc0_ref, kc0_ref, vc0_ref), (qc1_ref, kc1_ref, vc1_ref), (qc2_ref, kc2_ref, vc2_ref))
    for a, (lo, scale) in enumerate(((SEG_QC, A_SCALE), (SEG_KC, None), (SEG_VC, None))):
        z = seg(lo, 768)
        if scale is not None:
            z = z * scale
        c_out[0][a][...] = z[:, :256].astype(BF16)
        for g in (1, 2):
            dil = C_GROUPS[g][1]
            for c, blk in enumerate(_lane_blocks(256)):
                slot = (a * 2 + (g - 1)) * 2 + c
                stage_ref[slot] = z[:, g * 256 + c * LANE:g * 256 + (c + 1) * LANE]
                for r in range(dil):
                    c_out[g][a][r, :, blk] = stage_ref[slot, pl.ds(r, t // dil, stride=dil), :].astype(BF16)

    cosd, sind = cosd_ref[...], sind_ref[...]
    g64 = g64_ref[...]

    def head_norm(z, gain):
        sq = z * z
        hi = sq.astype(BF16)
        lo = (sq - hi.astype(F32)).astype(BF16)
        ms = _dot(hi, g64) + _dot(lo, g64)
        return z * lax.rsqrt(ms + EPS) * gain

    zq = seg(SEG_QD, 256)
    for blk in _lane_blocks(256):
        y = head_norm(zq[:, blk], gdq_ref[...])
        qd_ref[:, blk] = (_rope128(y, cosd, sind, first_half) * A_SCALE).astype(BF16)
    kd_ref[...] = _rope128(head_norm(seg(SEG_KD, 128), gdk_ref[...]), cosd, sind, first_half).astype(BF16)
    vd_ref[...] = (seg(SEG_VD, 256) + oned_ref[...]).astype(BF16)


def _proj(x, batch, seq_len, lw, tabs):
    n = x.shape[0]
    t = TILE_PROJ
    per_seq = seq_len // t
    row = lambda w: pl.BlockSpec((t, w), lambda i: (i, 0))
    tab = pl.BlockSpec((t, LANE), lambda i: (i % per_seq, 0))
    flat = lambda w: (row(w), jax.ShapeDtypeStruct((n, w), BF16))

    def planes(dil):
        return (pl.BlockSpec((None, dil, t // dil, 256), lambda i: (i // per_seq, 0, i % per_seq, 0)),
                jax.ShapeDtypeStruct((batch, dil, seq_len // dil, 256), BF16))

    outs = ([flat(D_MODEL), flat(256), flat(128), flat(128), flat(512), flat(512), flat(512)]
            + [flat(256)] * 3 + [planes(C_GROUPS[1][1])] * 3 + [planes(C_GROUPS[2][1])] * 3
            + [flat(256), flat(128), flat(256)])
    return pl.pallas_call(
        _proj_kernel,
        grid=(n // t,),
        in_specs=[row(D_MODEL), _const_spec((1, D_MODEL)), _const_spec((D_MODEL, IN_COLS_PACKED)),
                  _const_spec((MLA_Q_RANK, 512)), _const_spec((MLA_KV_RANK, 512)), _const_spec((MLA_KV_RANK, 512)),
                  _const_spec((1, MLA_Q_RANK)), _const_spec((1, MLA_KV_RANK)), _const_spec((1, LANE)),
                  _const_spec((1, LANE)), _const_spec((LANE, LANE)), _const_spec((1, 512)), _const_spec((1, 256)),
                  tab, tab, tab, tab],
        out_specs=[o[0] for o in outs],
        out_shape=[o[1] for o in outs],
        scratch_shapes=[pltpu.VMEM((12, t, LANE), F32)],
        compiler_params=_params("arbitrary"),
        name="norm_in_proj",
    )(x, lw["g_mix"], lw["w_in"], lw["w_uq"], lw["w_ukv_k"], lw["w_ukv_v"], lw["g_q"], lw["g_kv"], lw["g_dq"],
      lw["g_dk"], tabs["g64"], tabs["ones_b"], tabs["ones_d"], tabs["cos_b"], tabs["sin_b"], tabs["cos_d"],
      tabs["sin_d"])


def _build_bias(tab_ref, bkt_ref, bias_ref, n_heads, band):
    bkt = bkt_ref[...]
    row = lax.broadcasted_iota(jnp.int32, bkt.shape, 0)
    col = lax.broadcasted_iota(jnp.int32, bkt.shape, 1)
    valid = jnp.abs(col - band - row) <= band
    for hh in range(n_heads):
        def body(bb, acc, hh=hh):
            return jnp.where(bkt == bb, tab_ref[bb, hh], acc)
        acc = lax.fori_loop(0, N_BUCKETS, body, jnp.zeros(bkt.shape, F32))
        bias_ref[hh] = jnp.where(valid, acc * LOG2E, NEG)


def _banded_kernel(*refs, band, n_heads, paired_heads, has_sink):
    refs = list(refs)
    tab_ref = refs.pop(0)
    sink_ref = refs.pop(0) if has_sink else None
    bkt_ref, q_ref, kl_ref, kc_ref, kr_ref, vl_ref, vc_ref, vr_ref, o_ref = refs[:9]
    lse_ref = None if has_sink else refs[9]
    bias_ref, s_ref = refs[-2:]
    n = pl.program_id(1)
    last = pl.num_programs(1) - 1

    @pl.when((pl.program_id(0) == 0) & (n == 0))
    def _():
        _build_bias(tab_ref, bkt_ref, bias_ref, n_heads, band)

    t = BAND_TILE
    win = t + 2 * band
    n_sub = q_ref.shape[0] // t
    col = _lane_iota((t, win))
    is_lo = _lane_iota((1, LANE)) < HEAD_DIM
    zero = jnp.zeros((t, LANE), BF16)
    for blk in range(2):
        lanes = slice(blk * LANE, (blk + 1) * LANE)
        kv_lanes = slice(0, LANE) if paired_heads else lanes
        kall = jnp.concatenate([kl_ref[:, kv_lanes], kc_ref[:, kv_lanes], kr_ref[:, kv_lanes]], axis=0)
        for j in range(n_sub):
            qb = q_ref[j * t:(j + 1) * t, lanes]
            for half in range(2):
                qm = jnp.where(is_lo if half == 0 else jnp.logical_not(is_lo), qb, zero)
                s_ref[(blk * n_sub + j) * 2 + half] = _dot_nt(qm, kall[j * t:j * t + win])
    for blk in range(2):
        lanes = slice(blk * LANE, (blk + 1) * LANE)
        kv_lanes = slice(0, LANE) if paired_heads else lanes
        vall = jnp.concatenate([vl_ref[:, kv_lanes], vc_ref[:, kv_lanes], vr_ref[:, kv_lanes]], axis=0)
        for j in range(n_sub):
            rows = slice(j * t, (j + 1) * t)
            vwin = vall[j * t:j * t + win]
            outs, lses = [], []
            for half in range(2):
                head = blk + 2 * half if paired_heads else 2 * blk + half
                s = s_ref[(blk * n_sub + j) * 2 + half] + bias_ref[head]
                if j == 0:
                    s = jnp.where(col >= jnp.where(n > 0, 0, band), s, NEG)
                if j == n_sub - 1:
                    s = jnp.where(col < jnp.where(n < last, win, t + band), s, NEG)
                m = jnp.max(s, axis=-1, keepdims=True)
                if has_sink:
                    sk = sink_ref[head] * LOG2E
                    m = jnp.maximum(m, sk)
                p = jnp.exp2(s - m)
                denom = jnp.sum(p, axis=-1, keepdims=True)
                if has_sink:
                    denom = denom + jnp.exp2(sk - m)
                outs.append(_dot(p.astype(BF16), vwin) / denom)
                if lse_ref is not None:
                    lses.append(m * (1.0 / LOG2E) + jnp.log(denom))
            o_ref[rows, lanes] = jnp.where(is_lo, outs[0], outs[1]).astype(BF16)
            if lse_ref is not None:
                lse_ref[rows, lanes] = jnp.where(is_lo, lses[0], lses[1])


def _banded(q, k, v, table, bkt, band, sink=None):
    s, m, _ = q.shape
    kv_w = k.shape[-1]
    rows = min(BAND_STEP, m)
    per = rows // band
    nh = m // band
    halo = lambda f: pl.BlockSpec((None, band, kv_w), lambda si, ni: (si, f(ni), 0))
    left, right = (lambda ni: jnp.maximum(per * ni - 1, 0)), (lambda ni: jnp.minimum(per * (ni + 1), nh - 1))
    mid = lambda w: pl.BlockSpec((None, rows, w), lambda si, ni: (si, ni, 0))
    win = BAND_TILE + 2 * band
    has_sink = sink is not None
    out_bf = jax.ShapeDtypeStruct((s, m, 256), BF16)
    return pl.pallas_call(
        functools.partial(_banded_kernel, band=band, n_heads=table.shape[1], paired_heads=has_sink,
                          has_sink=has_sink),
        grid=(s, m // rows),
        in_specs=[_smem_spec()] + ([_smem_spec()] if has_sink else [])
                 + [_const_spec((BAND_TILE, win)), mid(256), halo(left), mid(kv_w), halo(right), halo(left),
                    mid(kv_w), halo(right)],
        out_specs=mid(256) if has_sink else [mid(256), mid(256)],
        out_shape=out_bf if has_sink else [out_bf, jax.ShapeDtypeStruct((s, m, 256), F32)],
        scratch_shapes=[pltpu.VMEM((table.shape[1], BAND_TILE, win), F32),
                        pltpu.VMEM((4 * (rows // BAND_TILE), BAND_TILE, win), F32)],
        compiler_params=_params("arbitrary", "arbitrary"),
        name="attn_window_sink" if has_sink else "attn_dilated",
    )(*([table] + ([sink] if has_sink else []) + [bkt, q, k, k, k, v, v, v]))


def _flash_chains(qs, k_refs, v_refs, s_ref, p_ref, m_ref, a_ref, acc_ref):
    nc = len(qs)
    rows = qs[0].shape[0]
    n_chunks = k_refs[0].shape[0] // FLASH_KV
    unroll = min(FLASH_UNROLL, n_chunks)
    ncol = FLASH_KV // LANE

    def scores(slot, chunk):
        start = pl.multiple_of(chunk * FLASH_KV, FLASH_KV)
        for c in range(nc):
            s_ref[slot, c] = _dot_nt(qs[c], k_refs[c][pl.ds(start, FLASH_KV), :])

    def consume(slot, chunk):
        start = pl.multiple_of(chunk * FLASH_KV, FLASH_KV)
        for c in range(nc):
            for r0 in range(0, rows, FLASH_SUB):
                rs = slice(r0, r0 + FLASH_SUB)
                sb = [s_ref[slot, c, rs, j * LANE:(j + 1) * LANE] for j in range(ncol)]
                smax = functools.reduce(jnp.maximum, sb)
                m_old = m_ref[c, rs, :]
                m_new = jnp.maximum(m_old, jnp.max(smax, axis=-1, keepdims=True))
                m_ref[c, rs, :] = m_new
                a_ref[c, rs, :] = jnp.exp2(m_old - m_new)
                for j in range(ncol):
                    p_ref[c, rs, j * LANE:(j + 1) * LANE] = jnp.exp2(sb[j] - m_new).astype(BF16)
            acc_ref[c] = a_ref[c] * acc_ref[c] + _dot(p_ref[c], v_refs[c][pl.ds(start, FLASH_KV), :])

    m_ref[...] = jnp.full(m_ref.shape, NEG, F32)
    acc_ref[...] = jnp.zeros(acc_ref.shape, F32)
    scores(0, 0)

    def trip(i, carry):
        base = i * unroll
        for u in range(unroll):
            scores((u + 1) % 2, base + u + 1)
            consume(u % 2, base + u)
        return carry

    lax.fori_loop(0, n_chunks // unroll - 1, trip, 0)
    base = n_chunks - unroll
    for u in range(unroll):
        if u + 1 < unroll:
            scores((u + 1) % 2, base + u + 1)
        consume(u % 2, base + u)
    return [acc_ref[c] / pltpu.roll(acc_ref[c], HEAD_DIM, 1) for c in range(nc)]


def _flash_scratch(nc, rows):
    stat = pltpu.VMEM((nc, rows, LANE), F32)
    return [pltpu.VMEM((2, nc, rows, FLASH_KV), F32), pltpu.VMEM((nc, rows, FLASH_KV), BF16), stat, stat, stat]


def _flash_b_kernel(qlo_ref, qhi_ref, klo_ref, khi_ref, vlo_ref, vhi_ref, o_ref, *scratch):
    o_lo, o_hi = _flash_chains([qlo_ref[...], qhi_ref[...]], [klo_ref, khi_ref], [vlo_ref, vhi_ref], *scratch)
    is_lo = _lane_iota((1, LANE)) < HEAD_DIM
    o_ref[...] = jnp.where(is_lo, o_lo, o_hi).astype(BF16)


def _flash_b(q, k, v):
    b, l, _ = q.shape
    tq = FLASH_ROWS
    qs = lambda off: pl.BlockSpec((None, tq, LANE), lambda bi, pi, qi: (bi, qi, 2 * pi + off))
    ks = lambda off: pl.BlockSpec((None, l, LANE), lambda bi, pi, qi: (bi, 0, 2 * pi + off))
    return pl.pallas_call(
        _flash_b_kernel,
        grid=(b, 2, l // tq),
        in_specs=[qs(0), qs(1), ks(0), ks(1), ks(0), ks(1)],
        out_specs=pl.BlockSpec((None, tq, LANE), lambda bi, pi, qi: (bi, qi, pi)),
        out_shape=jax.ShapeDtypeStruct((b, l, 256), BF16),
        scratch_shapes=_flash_scratch(2, FLASH_ROWS),
        compiler_params=_params("arbitrary", "arbitrary", "arbitrary"),
        name="attn_dense_mla",
    )(q, q, k, k, v, v)


def _flash_d_kernel(q_ref, k_ref, vlo_ref, vhi_ref, o_ref, *scratch):
    q = q_ref[...]
    tq = q.shape[0]
    is_lo = _lane_iota((1, LANE)) < HEAD_DIM
    zero = jnp.zeros((tq, LANE), BF16)
    qb = [q[:, :LANE], q[:, LANE:]]
    qs = [jnp.concatenate([jnp.where(keep, qb[0], zero), jnp.where(keep, qb[1], zero)], axis=0)
          for keep in (is_lo, jnp.logical_not(is_lo))]
    o_lo, o_hi = _flash_chains(qs, [k_ref, k_ref], [vlo_ref, vhi_ref], *scratch)
    o_ref[:, :LANE] = jnp.where(is_lo, o_lo[:tq], o_hi[:tq]).astype(BF16)
    o_ref[:, LANE:] = jnp.where(is_lo, o_lo[tq:], o_hi[tq:]).astype(BF16)


def _flash_d(q, k, v):
    b, l, _ = q.shape
    tq = FLASH_ROWS // 2
    whole = lambda blk: pl.BlockSpec((None, l, LANE), lambda bi, qi: (bi, 0, blk))
    tile = pl.BlockSpec((None, tq, 256), lambda bi, qi: (bi, qi, 0))
    return pl.pallas_call(
        _flash_d_kernel,
        grid=(b, l // tq),
        in_specs=[tile, whole(0), whole(0), whole(1)],
        out_specs=tile,
        out_shape=jax.ShapeDtypeStruct((b, l, 256), BF16),
        scratch_shapes=_flash_scratch(2, FLASH_ROWS),
        compiler_params=_params("arbitrary", "arbitrary"),
        name="attn_dense_axial",
    )(q, k, v, v)


def _merge_kernel(x_ref, h_ref, oa_ref, ob_ref, oc0_ref, l0_ref, oc1_ref, l1_ref, oc2_ref, l2_ref, od_ref,
                  wg_ref, wb_ref, wo_ref, gffn_ref, wrh_ref, wrl_ref, xo_ref, h2_ref, aff_ref, stage_ref):
    h = h_ref[...]
    t = h.shape[0]

    def token_order(ref, dil, slot):
        cols = []
        for c, blk in enumerate(_lane_blocks(256)):
            for r in range(dil):
                stage_ref[slot + c, pl.ds(r, t // dil, stride=dil), :] = ref[r, :, blk].astype(F32)
            cols.append(stage_ref[slot + c])
        return jnp.concatenate(cols, axis=1)

    o = [oc0_ref[...].astype(F32), token_order(oc1_ref, C_GROUPS[1][1], 0), token_order(oc2_ref, C_GROUPS[2][1], 2)]
    lse = [l0_ref[...], token_order(l1_ref, C_GROUPS[1][1], 4), token_order(l2_ref, C_GROUPS[2][1], 6)]
    mx = jnp.maximum(jnp.maximum(lse[0], lse[1]), lse[2])
    e = [jnp.exp(l - mx) for l in lse]
    den = e[0] + e[1] + e[2]
    oc = ((e[0] / den) * o[0] + (e[1] / den) * o[1] + (e[2] / den) * o[2]).astype(BF16)
    branches = (oa_ref[...], ob_ref[...], oc, od_ref[...])
    merged = None
    for bi in range(N_BRANCHES):
        term = jax.nn.sigmoid(_dot(h, wg_ref[bi])) * _dot(branches[bi], wb_ref[bi])
        merged = term if merged is None else merged + term
    xn = x_ref[...] + _dot(merged.astype(BF16), wo_ref[...])
    xo_ref[...] = xn
    hn = _rms(xn, gffn_ref[...])
    hi = hn.astype(BF16)
    h2_ref[...] = hi
    lo = (hn - hi.astype(F32)).astype(BF16)
    logits = _dot(hi, wrh_ref[...]) + _dot(lo, wrh_ref[...]) + _dot(hi, wrl_ref[...])
    logits = jnp.where(_lane_iota((1, LANE)) < N_EXPERTS, logits, NEG)
    ex = jnp.exp(logits - jnp.max(logits, axis=-1, keepdims=True))
    aff_ref[...] = ex / jnp.sum(ex, axis=-1, keepdims=True)


def _merge(x, h, oa, ob, oc, lse, od, seq_len, lw):
    n = x.shape[0]
    t = TILE_MERGE
    per_seq = seq_len // t
    row = lambda w: pl.BlockSpec((t, w), lambda i: (i, 0))
    planes = lambda dil: pl.BlockSpec((None, dil, t // dil, 256), lambda i: (i // per_seq, 0, i % per_seq, 0))
    p1, p2 = planes(C_GROUPS[1][1]), planes(C_GROUPS[2][1])
    return pl.pallas_call(
        _merge_kernel,
        grid=(n // t,),
        in_specs=[row(D_MODEL), row(D_MODEL), row(BRANCH_W), row(BRANCH_W), row(BRANCH_W), row(BRANCH_W),
                  p1, p1, p2, p2, row(BRANCH_W),
                  _const_spec((N_BRANCHES, D_MODEL, D_MODEL)), _const_spec((N_BRANCHES, BRANCH_W, D_MODEL)),
                  _const_spec((D_MODEL, D_MODEL)), _const_spec((1, D_MODEL)), _const_spec((D_MODEL, LANE)),
                  _const_spec((D_MODEL, LANE))],
        out_specs=[row(D_MODEL), row(D_MODEL), row(LANE)],
        out_shape=[jax.ShapeDtypeStruct((n, D_MODEL), F32), jax.ShapeDtypeStruct((n, D_MODEL), BF16),
                   jax.ShapeDtypeStruct((n, LANE), F32)],
        scratch_shapes=[pltpu.VMEM((8, t, LANE), F32)],
        compiler_params=_params("arbitrary"),
        name="gated_merge_router",
    )(x, h, oa, ob, oc[0], lse[0], oc[1], lse[1], oc[2], lse[2], od,
      lw["w_gate"], lw["w_branch"], lw["w_out"], lw["g_ffn"], lw["w_router_hi"], lw["w_router_lo"])


def _expert_kernel(xe_ref, gate_ref, wg_ref, wu_ref, wd_ref, ye_ref):
    xe = xe_ref[...]
    he = (jax.nn.silu(_dot(xe, wg_ref[...])) * _dot(xe, wu_ref[...])).astype(BF16)
    ye_ref[...] = _dot(he, wd_ref[...]) * gate_ref[...]


def _experts(xe, gate, lw):
    e, cap, d = xe.shape
    t = min(TILE_EXPERT, cap)
    ff = lw["w_exp_gate"].shape[-1]
    wspec = lambda a, b: pl.BlockSpec((None, a, b), lambda ei, ti: (ei, 0, 0))
    return pl.pallas_call(
        _expert_kernel,
        grid=(e, cap // t),
        in_specs=[pl.BlockSpec((None, t, d), lambda ei, ti: (ei, ti, 0)),
                  pl.BlockSpec((None, t, 1), lambda ei, ti: (ei, ti, 0)),
                  wspec(d, ff), wspec(d, ff), wspec(ff, d)],
        out_specs=pl.BlockSpec((None, t, d), lambda ei, ti: (ei, ti, 0)),
        out_shape=jax.ShapeDtypeStruct((e, cap, d), F32),
        compiler_params=_params("arbitrary", "arbitrary"),
        name="expert_swiglu",
    )(xe, gate, lw["w_exp_gate"], lw["w_exp_up"], lw["w_exp_down"])


def _combine_kernel(starts_ref, x_ref, tok_hbm, z_hbm, o_ref, tok_buf, z_buf, acc_ref, sem):
    b = pl.program_id(0)
    t = x_ref.shape[0]
    w = COMBINE_WIN
    base = lax.shift_left(lax.shift_right_logical(starts_ref[b], 7), 7)
    n_win = lax.div(starts_ref[b + 1] - base + (w - 1), w)

    def copies(i, slot):
        row0 = pl.multiple_of(base + i * w, LANE)
        return (pltpu.make_async_copy(tok_hbm.at[:, pl.ds(row0, w)], tok_buf.at[slot], sem.at[0, slot]),
                pltpu.make_async_copy(z_hbm.at[pl.ds(row0, w), :], z_buf.at[slot], sem.at[1, slot]))

    @pl.when(n_win > 0)
    def _():
        for c in copies(0, 0):
            c.start()

    acc_ref[...] = jnp.zeros(acc_ref.shape, F32)
    ids = b * t + lax.broadcasted_iota(jnp.int32, (t, 1), 0)

    def window(i, carry):
        slot = lax.rem(i, 2)
        for c in copies(i, slot):
            c.wait()

        @pl.when(i + 1 < n_win)
        def _():
            for c in copies(i + 1, 1 - slot):
                c.start()

        onehot = jnp.where(tok_buf[slot] == ids, 1.0, 0.0).astype(BF16)
        z = z_buf[slot]
        hi = z.astype(BF16)
        lo = (z - hi.astype(F32)).astype(BF16)
        acc_ref[...] += _dot(onehot, hi) + _dot(onehot, lo)
        return carry

    lax.fori_loop(0, n_win, window, 0)
    o_ref[...] = x_ref[...] + acc_ref[...]


def _combine(x, tok_sorted, z, starts):
    n, d = x.shape
    t = TILE_COMBINE
    w = COMBINE_WIN
    return pl.pallas_call(
        _combine_kernel,
        grid_spec=pltpu.PrefetchScalarGridSpec(
            num_scalar_prefetch=1,
            grid=(n // t,),
            in_specs=[pl.BlockSpec((t, d), lambda i, s: (i, 0)), pl.BlockSpec(memory_space=pl.ANY),
                      pl.BlockSpec(memory_space=pl.ANY)],
            out_specs=pl.BlockSpec((t, d), lambda i, s: (i, 0)),
            scratch_shapes=[pltpu.VMEM((2, 1, w), jnp.int32), pltpu.VMEM((2, w, d), F32), pltpu.VMEM((t, d), F32),
                            pltpu.SemaphoreType.DMA((2, 2))],
        ),
        out_shape=jax.ShapeDtypeStruct((n, d), F32),
        compiler_params=_params("arbitrary"),
        name="expert_combine",
    )(starts, x, tok_sorted, z)


def _final_kernel(x_ref, g_ref, o_ref):
    o_ref[...] = _rms(x_ref[...], g_ref[...])


def _final_norm(x, g):
    n = x.shape[0]
    t = TILE_PROJ
    return pl.pallas_call(
        _final_kernel,
        grid=(n // t,),
        in_specs=[pl.BlockSpec((t, D_MODEL), lambda i: (i, 0)), _const_spec((1, D_MODEL))],
        out_specs=pl.BlockSpec((t, D_MODEL), lambda i: (i, 0)),
        out_shape=jax.ShapeDtypeStruct((n, D_MODEL), F32),
        compiler_params=_params("arbitrary"),
        name="final_norm",
    )(x, g)


def _t5_bucket_np(rel):
    nb = N_BUCKETS // 2
    max_exact = nb // 2
    n = np.abs(rel)
    large = max_exact + (np.log(np.maximum(n, 1).astype(np.float64) / max_exact)
                         / math.log(MAX_DISTANCE / max_exact) * (nb - max_exact)).astype(np.int32)
    large = np.minimum(large, nb - 1)
    return (np.where(rel > 0, nb, 0) + np.where(n < max_exact, n, large)).astype(np.int32)


def _band_buckets(rows, band, dil):
    rel = np.arange(rows + 2 * band, dtype=np.int32)[None, :] - band - np.arange(rows, dtype=np.int32)[:, None]
    return jnp.asarray(_t5_bucket_np(rel * dil))


def _position_tables(seq_len):
    t = jnp.arange(seq_len, dtype=jnp.int32)
    inv = ROPE_THETA ** (-jnp.arange(ROPE_HALF, dtype=F32) / ROPE_HALF)

    def cs(pos):
        ang = pos.astype(F32)[:, None] * inv[None, :]
        return jnp.cos(ang), jnp.sin(ang)

    def group(c, s):
        return jnp.concatenate([c, c], axis=1), jnp.concatenate([-s, s], axis=1)

    ones, zeros = jnp.ones((seq_len, 32), F32), jnp.zeros((seq_len, 32), F32)
    cb, sb = group(*cs(t))
    cr, sr = group(*cs(t // GRID_W))
    cc, sc = group(*cs(t % GRID_W))
    g64 = np.kron(np.eye(2, dtype=np.float32), np.full((HEAD_DIM, HEAD_DIM), 1.0 / HEAD_DIM, np.float32))
    pad_ones = np.tile(np.concatenate([np.zeros(64), np.ones(128), np.zeros(64)]).astype(np.float32), 2)
    return {
        "cos_b": jnp.concatenate([ones, ones, cb, ones], axis=1),
        "sin_b": jnp.concatenate([zeros, zeros, sb, zeros], axis=1),
        "cos_d": jnp.concatenate([cr, cc, cr, cc], axis=1),
        "sin_d": jnp.concatenate([sr, sc, sr, sc], axis=1),
        "g64": jnp.asarray(g64, BF16),
        "ones_b": jnp.asarray(pad_ones.reshape(1, 512)),
        "ones_d": jnp.asarray(pad_ones[:256].reshape(1, 256)),
    }


def _pad_value_heads(w):
    d, heads, _ = w.shape
    z = jnp.zeros((d, heads // 2, HEAD_DIM), w.dtype)
    return jnp.stack([w[:, 0::2], z, z, w[:, 1::2]], axis=2).reshape(d, heads * LANE)


def _pack_layer(l, p):
    w_in = p["w_in"][l]
    split = np.cumsum([0, 256, 128, 128, 256, 128, 32, 768, 768, 768, 256, 128, 128])
    qa, ka, va, cq, ckv, kr, qc, kc, vc, qd, kd, vd = [w_in[:, split[i]:split[i + 1]] for i in range(12)]

    def pair_heads(w):
        d = w.shape[0]
        return w.reshape(d, 4, HEAD_DIM)[:, jnp.array([0, 2, 1, 3])].reshape(d, 256)

    z = lambda w: jnp.zeros((w_in.shape[0], w), F32)
    packed = jnp.concatenate([pair_heads(qa), ka, va, cq, ckv, z(64), kr, z(32), qc, kc, vc, pair_heads(qd), kd,
                              _pad_value_heads(vd.reshape(-1, 2, HEAD_DIM))], axis=1)
    uq = p["mla_w_uq"][l].reshape(MLA_Q_RANK, MLA_HEADS, MLA_NOPE + MLA_ROPE)
    uq = jnp.pad(uq, ((0, 0), (0, 0), (0, LANE - MLA_NOPE - MLA_ROPE))).reshape(MLA_Q_RANK, MLA_HEADS * LANE)
    ukv = p["mla_w_ukv"][l].reshape(MLA_KV_RANK, MLA_HEADS, MLA_NOPE + MLA_V)
    ukv_k = jnp.pad(ukv[:, :, :MLA_NOPE], ((0, 0), (0, 0), (0, LANE - MLA_NOPE))).reshape(MLA_KV_RANK, MLA_HEADS * LANE)
    ukv_v = _pad_value_heads(ukv[:, :, MLA_NOPE:])
    wb = p["w_branch"][l]
    pair_rows = lambda w: w.reshape(4, HEAD_DIM, D_MODEL)[jnp.array([0, 2, 1, 3])].reshape(BRANCH_W, D_MODEL)
    wb = jnp.stack([pair_rows(wb[0]), wb[1], wb[2], pair_rows(wb[3])])
    wr = jnp.pad(p["w_router"][l], ((0, 0), (0, LANE - N_EXPERTS)))
    wr_hi = wr.astype(BF16)
    row = lambda g: g.reshape(1, -1).astype(F32)
    return {
        "g_mix": row(p["norm_mix"][l]), "w_in": packed.astype(BF16), "w_uq": uq.astype(BF16),
        "w_ukv_k": ukv_k.astype(BF16), "w_ukv_v": ukv_v.astype(BF16),
        "g_q": row(p["mla_q_norm"][l]), "g_kv": row(p["mla_kv_norm"][l]),
        "g_dq": row(jnp.tile(p["d_q_norm"][l], 2)), "g_dk": row(jnp.tile(p["d_k_norm"][l], 2)),
        "a_sink": p["a_sink"][l].astype(F32),
        "w_gate": p["w_gate"][l].astype(BF16), "w_branch": wb.astype(BF16), "w_out": p["w_out"][l].astype(BF16),
        "g_ffn": row(p["norm_ffn"][l]), "w_router_hi": wr_hi, "w_router_lo": (wr - wr_hi.astype(F32)).astype(BF16),
        "w_exp_gate": p["w_exp_gate"][l].astype(BF16), "w_exp_up": p["w_exp_up"][l].astype(BF16),
        "w_exp_down": p["w_exp_down"][l].astype(BF16),
    }


def _layer(x, b, l, lw, tabs, t5_table):
    n = b * l
    (h, qa, ka, va, qb, kb, vb, qc0, kc0, vc0, qc1, kc1, vc1, qc2, kc2, vc2, qd, kd, vd) = _proj(x, b, l, lw, tabs)
    seq = lambda a: a.reshape(b, l, a.shape[-1])
    flat = lambda a: a.reshape(n, a.shape[-1])

    oa = flat(_banded(seq(qa), seq(ka), seq(va), t5_table[:, :A_HEADS], tabs["bkt_a"], A_WINDOW, lw["a_sink"]))
    ob = flat(_flash_b(seq(qb), seq(kb), seq(vb)))
    od = flat(_flash_d(seq(qd), seq(kd), seq(vd)))

    oc, lse = [], []
    for g, (q, k, v) in enumerate(((seq(qc0), seq(kc0), seq(vc0)), (qc1, kc1, vc1), (qc2, kc2, vc2))):
        dil = C_GROUPS[g][1]
        table = t5_table[:, A_HEADS + g * C_HEADS:A_HEADS + (g + 1) * C_HEADS]
        streams = lambda a: a.reshape(b * dil, l // dil, 256)
        o_g, lse_g = _banded(streams(q), streams(k), streams(v), table, tabs["bkt_c"][g], C_BAND)
        if dil == 1:
            oc.append(flat(o_g))
            lse.append(flat(lse_g))
        else:
            oc.append(o_g.reshape(b, dil, l // dil, 256))
            lse.append(lse_g.reshape(b, dil, l // dil, 256))

    x_mid, h2, aff = _merge(x, h, oa, ob, oc, lse, od, l, lw)

    cap = EC_CAPACITY_FACTOR * n // N_EXPERTS
    gate, idx = lax.top_k(aff[:, :N_EXPERTS].T, cap)
    ye = _experts(h2[idx], gate[..., None], lw)

    pair_tok = idx.reshape(-1)
    order = jnp.argsort(pair_tok).astype(jnp.int32)
    pad = COMBINE_WIN + LANE
    tok_sorted = jnp.concatenate([pair_tok[order], jnp.full((pad,), -1, jnp.int32)])
    z = ye.reshape(-1, D_MODEL)[jnp.concatenate([order, jnp.zeros((pad,), jnp.int32)])]
    starts = jnp.searchsorted(tok_sorted[:pair_tok.shape[0]], jnp.arange(0, n + 1, TILE_COMBINE, dtype=jnp.int32))
    return _combine(x_mid, tok_sorted[None, :], z, starts.astype(jnp.int32))


def _trunk(x, layers, t5_table, norm_final):
    b, l, _ = x.shape
    tabs = _position_tables(l)
    tabs["bkt_a"] = _band_buckets(A_WINDOW, A_WINDOW, 1)
    tabs["bkt_c"] = [_band_buckets(2 * C_BAND, C_BAND, dil) for _, dil in C_GROUPS]
    x = x.reshape(b * l, D_MODEL)
    for lw in layers:
        x = _layer(x, b, l, lw, tabs, t5_table)
    return _final_norm(x, norm_final.reshape(1, -1)).reshape(b, l, D_MODEL)


def kernel(x_prompt, x_sample, t5_table, norm_mix, w_in, a_sink, mla_q_norm, mla_w_uq, mla_kv_norm, mla_w_ukv,
           d_q_norm, d_k_norm, w_gate, w_branch, w_out, norm_ffn, w_router, w_exp_gate, w_exp_up, w_exp_down,
           norm_final):
    p = dict(norm_mix=norm_mix, w_in=w_in, a_sink=a_sink, mla_q_norm=mla_q_norm, mla_w_uq=mla_w_uq,
             mla_kv_norm=mla_kv_norm, mla_w_ukv=mla_w_ukv, d_q_norm=d_q_norm, d_k_norm=d_k_norm, w_gate=w_gate,
             w_branch=w_branch, w_out=w_out, norm_ffn=norm_ffn, w_router=w_router, w_exp_gate=w_exp_gate,
             w_exp_up=w_exp_up, w_exp_down=w_exp_down)
    layers = [_pack_layer(l, p) for l in range(w_in.shape[0])]
    t5 = t5_table.astype(F32)
    return (_trunk(x_prompt, layers, t5, norm_final), _trunk(x_sample, layers, t5, norm_final))
```

```python
import functools
import math

import numpy as np
import jax
import jax.numpy as jnp
from jax import lax
from jax.experimental import pallas as pl
from jax.experimental.pallas import tpu as pltpu

F32 = jnp.float32
BF16 = jnp.bfloat16

D_MODEL = 1024
HEAD_DIM = 64
A_HEADS = 4
A_WINDOW = 128
MLA_HEADS = 4
MLA_Q_RANK = 256
MLA_KV_RANK = 128
MLA_NOPE = 64
MLA_ROPE = 32
MLA_V = 64
C_HEADS = 4
C_GROUPS = ((128, 1), (512, 4), (2048, 16))
C_BAND = 64
GRID_W = 64
N_BRANCHES = 4
BRANCH_W = 256
N_BUCKETS = 32
MAX_DISTANCE = 1024
N_EXPERTS = 16
EC_CAPACITY_FACTOR = 2
ROPE_THETA = 10000.0
ROPE_HALF = 16
EPS = 1e-6
NEG = -1e30

LANE = 128
VMEM_LIMIT = 56 * 1024 * 1024

SEG_QA, SEG_KA, SEG_VA = 0, 256, 384
SEG_CQ, SEG_CKV, SEG_KR = 512, 768, 896
SEG_QC, SEG_KC, SEG_VC = 1024, 1792, 2560
SEG_QD, SEG_KD, SEG_VD = 3328, 3584, 3712
IN_COLS_PACKED = 3968

LOG2E = math.log2(math.e)
A_SCALE = HEAD_DIM ** -0.5 * LOG2E
B_SCALE = (MLA_NOPE + MLA_ROPE) ** -0.5 * LOG2E

TILE_PROJ = 512
TILE_MERGE = 256
TILE_EXPERT = 512
TILE_COMBINE = 256
COMBINE_WIN = 768
BAND_TILE = 128
BAND_STEP = 512
FLASH_ROWS = 256
FLASH_KV = 512
FLASH_SUB = 32
FLASH_UNROLL = 8


def _dot(a, b):
    return jnp.dot(a, b, preferred_element_type=F32)


def _dot_nt(a, b):
    return lax.dot_general(a, b, (((1,), (1,)), ((), ())), preferred_element_type=F32)


def _params(*sem):
    return pltpu.CompilerParams(dimension_semantics=sem, vmem_limit_bytes=VMEM_LIMIT)


def _const_spec(shape):
    zeros = (0,) * len(shape)
    return pl.BlockSpec(shape, lambda *_: zeros, pipeline_mode=pl.Buffered(1))


def _smem_spec():
    return pl.BlockSpec(memory_space=pltpu.SMEM)


def _lane_iota(shape):
    return lax.broadcasted_iota(jnp.int32, shape, len(shape) - 1)


def _rope128(x, cos, sin_signed, first_half):
    rot = jnp.where(first_half, pltpu.roll(x, LANE - ROPE_HALF, 1), pltpu.roll(x, ROPE_HALF, 1))
    return x * cos + rot * sin_signed


def _rms(x, gain):
    return x * lax.rsqrt(jnp.mean(x * x, axis=-1, keepdims=True) + EPS) * gain


def _lane_blocks(width):
    return [slice(c * LANE, (c + 1) * LANE) for c in range(width // LANE)]


def _proj_kernel(x_ref, gmix_ref, win_ref, wuq_ref, wukvk_ref, wukvv_ref, gq_ref, gkv_ref, gdq_ref, gdk_ref,
                 g64_ref, oneb_ref, oned_ref, cosb_ref, sinb_ref, cosd_ref, sind_ref,
                 h_ref, qa_ref, ka_ref, va_ref, qb_ref, kb_ref, vb_ref,
                 qc0_ref, kc0_ref, vc0_ref, qc1_ref, kc1_ref, vc1_ref, qc2_ref, kc2_ref, vc2_ref,
                 qd_ref, kd_ref, vd_ref, stage_ref):
    h = _rms(x_ref[...], gmix_ref[...]).astype(BF16)
    h_ref[...] = h

    def seg(lo, width):
        return _dot(h, win_ref[:, lo:lo + width])

    first_half = jnp.bitwise_and(_lane_iota((1, LANE)), 2 * ROPE_HALF - 1) < ROPE_HALF

    qa_ref[...] = (seg(SEG_QA, 256) * A_SCALE).astype(BF16)
    ka_ref[...] = seg(SEG_KA, 128).astype(BF16)
    va_ref[...] = seg(SEG_VA, 128).astype(BF16)

    cosb, sinb = cosb_ref[...], sinb_ref[...]
    cqn = _rms(seg(SEG_CQ, MLA_Q_RANK), gq_ref[...]).astype(BF16)
    qm = _dot(cqn, wuq_ref[...])
    for blk in _lane_blocks(MLA_HEADS * LANE):
        qb_ref[:, blk] = (_rope128(qm[:, blk], cosb, sinb, first_half) * B_SCALE).astype(BF16)
    ckvn = _rms(seg(SEG_CKV, MLA_KV_RANK), gkv_ref[...]).astype(BF16)
    k_pe = _rope128(seg(SEG_KR, LANE), cosb, sinb, first_half)
    kn = _dot(ckvn, wukvk_ref[...])
    for blk in _lane_blocks(MLA_HEADS * LANE):
        kb_ref[:, blk] = (kn[:, blk] + k_pe).astype(BF16)
    vb_ref[...] = (_dot(ckvn, wukvv_ref[...]) + oneb_ref[...]).astype(BF16)

    t = x_ref.shape[0]
    c_out = ((qc0_ref, kc0_ref, vc0_ref), (qc1_ref, kc1_ref, vc1_ref), (qc2_ref, kc2_ref, vc2_ref))
    for a, (lo, scale) in enumerate(((SEG_QC, A_SCALE), (SEG_KC, None), (SEG_VC, None))):
        z = seg(lo, 768)
        if scale is not None:
            z = z * scale
        c_out[0][a][...] = z[:, :256].astype(BF16)
        for g in (1, 2):
            dil = C_GROUPS[g][1]
            for c, blk in enumerate(_lane_blocks(256)):
                slot = (a * 2 + (g - 1)) * 2 + c
                stage_ref[slot] = z[:, g * 256 + c * LANE:g * 256 + (c + 1) * LANE]
                for r in range(dil):
                    c_out[g][a][r, :, blk] = stage_ref[slot, pl.ds(r, t // dil, stride=dil), :].astype(BF16)

    cosd, sind = cosd_ref[...], sind_ref[...]
    g64 = g64_ref[...]

    def head_norm(z, gain):
        sq = z * z
        hi = sq.astype(BF16)
        lo = (sq - hi.astype(F32)).astype(BF16)
        ms = _dot(hi, g64) + _dot(lo, g64)
        return z * lax.rsqrt(ms + EPS) * gain

    zq = seg(SEG_QD, 256)
    for blk in _lane_blocks(256):
        y = head_norm(zq[:, blk], gdq_ref[...])
        qd_ref[:, blk] = (_rope128(y, cosd, sind, first_half) * A_SCALE).astype(BF16)
    kd_ref[...] = _rope128(head_norm(seg(SEG_KD, 128), gdk_ref[...]), cosd, sind, first_half).astype(BF16)
    vd_ref[...] = (seg(SEG_VD, 256) + oned_ref[...]).astype(BF16)


def _proj(x, batch, seq_len, lw, tabs):
    n = x.shape[0]
    t = TILE_PROJ
    per_seq = seq_len // t
    row = lambda w: pl.BlockSpec((t, w), lambda i: (i, 0))
    tab = pl.BlockSpec((t, LANE), lambda i: (i % per_seq, 0))
    flat = lambda w: (row(w), jax.ShapeDtypeStruct((n, w), BF16))

    def planes(dil):
        return (pl.BlockSpec((None, dil, t // dil, 256), lambda i: (i // per_seq, 0, i % per_seq, 0)),
                jax.ShapeDtypeStruct((batch, dil, seq_len // dil, 256), BF16))

    outs = ([flat(D_MODEL), flat(256), flat(128), flat(128), flat(512), flat(512), flat(512)]
            + [flat(256)] * 3 + [planes(C_GROUPS[1][1])] * 3 + [planes(C_GROUPS[2][1])] * 3
            + [flat(256), flat(128), flat(256)])
    return pl.pallas_call(
        _proj_kernel,
        grid=(n // t,),
        in_specs=[row(D_MODEL), _const_spec((1, D_MODEL)), _const_spec((D_MODEL, IN_COLS_PACKED)),
                  _const_spec((MLA_Q_RANK, 512)), _const_spec((MLA_KV_RANK, 512)), _const_spec((MLA_KV_RANK, 512)),
                  _const_spec((1, MLA_Q_RANK)), _const_spec((1, MLA_KV_RANK)), _const_spec((1, LANE)),
                  _const_spec((1, LANE)), _const_spec((LANE, LANE)), _const_spec((1, 512)), _const_spec((1, 256)),
                  tab, tab, tab, tab],
        out_specs=[o[0] for o in outs],
        out_shape=[o[1] for o in outs],
        scratch_shapes=[pltpu.VMEM((12, t, LANE), F32)],
        compiler_params=_params("arbitrary"),
        name="norm_in_proj",
    )(x, lw["g_mix"], lw["w_in"], lw["w_uq"], lw["w_ukv_k"], lw["w_ukv_v"], lw["g_q"], lw["g_kv"], lw["g_dq"],
      lw["g_dk"], tabs["g64"], tabs["ones_b"], tabs["ones_d"], tabs["cos_b"], tabs["sin_b"], tabs["cos_d"],
      tabs["sin_d"])


def _build_bias(tab_ref, bkt_ref, bias_ref, n_heads, band):
    bkt = bkt_ref[...]
    row = lax.broadcasted_iota(jnp.int32, bkt.shape, 0)
    col = lax.broadcasted_iota(jnp.int32, bkt.shape, 1)
    valid = jnp.abs(col - band - row) <= band
    for hh in range(n_heads):
        def body(bb, acc, hh=hh):
            return jnp.where(bkt == bb, tab_ref[bb, hh], acc)
        acc = lax.fori_loop(0, N_BUCKETS, body, jnp.zeros(bkt.shape, F32))
        bias_ref[hh] = jnp.where(valid, acc * LOG2E, NEG)


def _banded_kernel(*refs, band, n_heads, paired_heads, has_sink):
    refs = list(refs)
    tab_ref = refs.pop(0)
    sink_ref = refs.pop(0) if has_sink else None
    bkt_ref, q_ref, kl_ref, kc_ref, kr_ref, vl_ref, vc_ref, vr_ref, o_ref = refs[:9]
    lse_ref = None if has_sink else refs[9]
    bias_ref, s_ref = refs[-2:]
    n = pl.program_id(1)
    last = pl.num_programs(1) - 1

    @pl.when((pl.program_id(0) == 0) & (n == 0))
    def _():
        _build_bias(tab_ref, bkt_ref, bias_ref, n_heads, band)

    t = BAND_TILE
    win = t + 2 * band
    n_sub = q_ref.shape[0] // t
    col = _lane_iota((t, win))
    is_lo = _lane_iota((1, LANE)) < HEAD_DIM
    zero = jnp.zeros((t, LANE), BF16)
    for blk in range(2):
        lanes = slice(blk * LANE, (blk + 1) * LANE)
        kv_lanes = slice(0, LANE) if paired_heads else lanes
        kall = jnp.concatenate([kl_ref[:, kv_lanes], kc_ref[:, kv_lanes], kr_ref[:, kv_lanes]], axis=0)
        for j in range(n_sub):
            qb = q_ref[j * t:(j + 1) * t, lanes]
            for half in range(2):
                qm = jnp.where(is_lo if half == 0 else jnp.logical_not(is_lo), qb, zero)
                s_ref[(blk * n_sub + j) * 2 + half] = _dot_nt(qm, kall[j * t:j * t + win])
    for blk in range(2):
        lanes = slice(blk * LANE, (blk + 1) * LANE)
        kv_lanes = slice(0, LANE) if paired_heads else lanes
        vall = jnp.concatenate([vl_ref[:, kv_lanes], vc_ref[:, kv_lanes], vr_ref[:, kv_lanes]], axis=0)
        for j in range(n_sub):
            rows = slice(j * t, (j + 1) * t)
            vwin = vall[j * t:j * t + win]
            outs, lses = [], []
            for half in range(2):
                head = blk + 2 * half if paired_heads else 2 * blk + half
                s = s_ref[(blk * n_sub + j) * 2 + half] + bias_ref[head]
                if j == 0:
                    s = jnp.where(col >= jnp.where(n > 0, 0, band), s, NEG)
                if j == n_sub - 1:
                    s = jnp.where(col < jnp.where(n < last, win, t + band), s, NEG)
                m = jnp.max(s, axis=-1, keepdims=True)
                if has_sink:
                    sk = sink_ref[head] * LOG2E
                    m = jnp.maximum(m, sk)
                p = jnp.exp2(s - m)
                denom = jnp.sum(p, axis=-1, keepdims=True)
                if has_sink:
                    denom = denom + jnp.exp2(sk - m)
                outs.append(_dot(p.astype(BF16), vwin) / denom)
                if lse_ref is not None:
                    lses.append(m * (1.0 / LOG2E) + jnp.log(denom))
            o_ref[rows, lanes] = jnp.where(is_lo, outs[0], outs[1]).astype(BF16)
            if lse_ref is not None:
                lse_ref[rows, lanes] = jnp.where(is_lo, lses[0], lses[1])


def _banded(q, k, v, table, bkt, band, sink=None):
    s, m, _ = q.shape
    kv_w = k.shape[-1]
    rows = min(BAND_STEP, m)
    per = rows // band
    nh = m // band
    halo = lambda f: pl.BlockSpec((None, band, kv_w), lambda si, ni: (si, f(ni), 0))
    left, right = (lambda ni: jnp.maximum(per * ni - 1, 0)), (lambda ni: jnp.minimum(per * (ni + 1), nh - 1))
    mid = lambda w: pl.BlockSpec((None, rows, w), lambda si, ni: (si, ni, 0))
    win = BAND_TILE + 2 * band
    has_sink = sink is not None
    out_bf = jax.ShapeDtypeStruct((s, m, 256), BF16)
    return pl.pallas_call(
        functools.partial(_banded_kernel, band=band, n_heads=table.shape[1], paired_heads=has_sink,
                          has_sink=has_sink),
        grid=(s, m // rows),
        in_specs=[_smem_spec()] + ([_smem_spec()] if has_sink else [])
                 + [_const_spec((BAND_TILE, win)), mid(256), halo(left), mid(kv_w), halo(right), halo(left),
                    mid(kv_w), halo(right)],
        out_specs=mid(256) if has_sink else [mid(256), mid(256)],
        out_shape=out_bf if has_sink else [out_bf, jax.ShapeDtypeStruct((s, m, 256), F32)],
        scratch_shapes=[pltpu.VMEM((table.shape[1], BAND_TILE, win), F32),
                        pltpu.VMEM((4 * (rows // BAND_TILE), BAND_TILE, win), F32)],
        compiler_params=_params("arbitrary", "arbitrary"),
        name="attn_window_sink" if has_sink else "attn_dilated",
    )(*([table] + ([sink] if has_sink else []) + [bkt, q, k, k, k, v, v, v]))


def _flash_chains(qs, k_refs, v_refs, s_ref, p_ref, m_ref, a_ref, acc_ref):
    nc = len(qs)
    rows = qs[0].shape[0]
    n_chunks = k_refs[0].shape[0] // FLASH_KV
    unroll = min(FLASH_UNROLL, n_chunks)
    ncol = FLASH_KV // LANE

    def scores(slot, chunk):
        start = pl.multiple_of(chunk * FLASH_KV, FLASH_KV)
        for c in range(nc):
            s_ref[slot, c] = _dot_nt(qs[c], k_refs[c][pl.ds(start, FLASH_KV), :])

    def consume(slot, chunk):
        start = pl.multiple_of(chunk * FLASH_KV, FLASH_KV)
        for c in range(nc):
            for r0 in range(0, rows, FLASH_SUB):
                rs = slice(r0, r0 + FLASH_SUB)
                sb = [s_ref[slot, c, rs, j * LANE:(j + 1) * LANE] for j in range(ncol)]
                smax = functools.reduce(jnp.maximum, sb)
                m_old = m_ref[c, rs, :]
                m_new = jnp.maximum(m_old, jnp.max(smax, axis=-1, keepdims=True))
                m_ref[c, rs, :] = m_new
                a_ref[c, rs, :] = jnp.exp2(m_old - m_new)
                for j in range(ncol):
                    p_ref[c, rs, j * LANE:(j + 1) * LANE] = jnp.exp2(sb[j] - m_new).astype(BF16)
            acc_ref[c] = a_ref[c] * acc_ref[c] + _dot(p_ref[c], v_refs[c][pl.ds(start, FLASH_KV), :])

    m_ref[...] = jnp.full(m_ref.shape, NEG, F32)
    acc_ref[...] = jnp.zeros(acc_ref.shape, F32)
    scores(0, 0)

    def trip(i, carry):
        base = i * unroll
        for u in range(unroll):
            scores((u + 1) % 2, base + u + 1)
            consume(u % 2, base + u)
        return carry

    lax.fori_loop(0, n_chunks // unroll - 1, trip, 0)
    base = n_chunks - unroll
    for u in range(unroll):
        if u + 1 < unroll:
            scores((u + 1) % 2, base + u + 1)
        consume(u % 2, base + u)
    return [acc_ref[c] / pltpu.roll(acc_ref[c], HEAD_DIM, 1) for c in range(nc)]


def _flash_scratch(nc, rows):
    stat = pltpu.VMEM((nc, rows, LANE), F32)
    return [pltpu.VMEM((2, nc, rows, FLASH_KV), F32), pltpu.VMEM((nc, rows, FLASH_KV), BF16), stat, stat, stat]


def _flash_b_kernel(qlo_ref, qhi_ref, klo_ref, khi_ref, vlo_ref, vhi_ref, o_ref, *scratch):
    o_lo, o_hi = _flash_chains([qlo_ref[...], qhi_ref[...]], [klo_ref, khi_ref], [vlo_ref, vhi_ref], *scratch)
    is_lo = _lane_iota((1, LANE)) < HEAD_DIM
    o_ref[...] = jnp.where(is_lo, o_lo, o_hi).astype(BF16)


def _flash_b(q, k, v):
    b, l, _ = q.shape
    tq = FLASH_ROWS
    qs = lambda off: pl.BlockSpec((None, tq, LANE), lambda bi, pi, qi: (bi, qi, 2 * pi + off))
    ks = lambda off: pl.BlockSpec((None, l, LANE), lambda bi, pi, qi: (bi, 0, 2 * pi + off))
    return pl.pallas_call(
        _flash_b_kernel,
        grid=(b, 2, l // tq),
        in_specs=[qs(0), qs(1), ks(0), ks(1), ks(0), ks(1)],
        out_specs=pl.BlockSpec((None, tq, LANE), lambda bi, pi, qi: (bi, qi, pi)),
        out_shape=jax.ShapeDtypeStruct((b, l, 256), BF16),
        scratch_shapes=_flash_scratch(2, FLASH_ROWS),
        compiler_params=_params("arbitrary", "arbitrary", "arbitrary"),
        name="attn_dense_mla",
    )(q, q, k, k, v, v)


def _flash_d_kernel(q_ref, k_ref, vlo_ref, vhi_ref, o_ref, *scratch):
    q = q_ref[...]
    tq = q.shape[0]
    is_lo = _lane_iota((1, LANE)) < HEAD_DIM
    zero = jnp.zeros((tq, LANE), BF16)
    qb = [q[:, :LANE], q[:, LANE:]]
    qs = [jnp.concatenate([jnp.where(keep, qb[0], zero), jnp.where(keep, qb[1], zero)], axis=0)
          for keep in (is_lo, jnp.logical_not(is_lo))]
    o_lo, o_hi = _flash_chains(qs, [k_ref, k_ref], [vlo_ref, vhi_ref], *scratch)
    o_ref[:, :LANE] = jnp.where(is_lo, o_lo[:tq], o_hi[:tq]).astype(BF16)
    o_ref[:, LANE:] = jnp.where(is_lo, o_lo[tq:], o_hi[tq:]).astype(BF16)


def _flash_d(q, k, v):
    b, l, _ = q.shape
    tq = FLASH_ROWS // 2
    whole = lambda blk: pl.BlockSpec((None, l, LANE), lambda bi, qi: (bi, 0, blk))
    tile = pl.BlockSpec((None, tq, 256), lambda bi, qi: (bi, qi, 0))
    return pl.pallas_call(
        _flash_d_kernel,
        grid=(b, l // tq),
        in_specs=[tile, whole(0), whole(0), whole(1)],
        out_specs=tile,
        out_shape=jax.ShapeDtypeStruct((b, l, 256), BF16),
        scratch_shapes=_flash_scratch(2, FLASH_ROWS),
        compiler_params=_params("arbitrary", "arbitrary"),
        name="attn_dense_axial",
    )(q, k, v, v)


def _merge_kernel(x_ref, h_ref, oa_ref, ob_ref, oc0_ref, l0_ref, oc1_ref, l1_ref, oc2_ref, l2_ref, od_ref,
                  wg_ref, wb_ref, wo_ref, gffn_ref, wrh_ref, wrl_ref, xo_ref, h2_ref, aff_ref, stage_ref):
    h = h_ref[...]
    t = h.shape[0]

    def token_order(ref, dil, slot):
        cols = []
        for c, blk in enumerate(_lane_blocks(256)):
            for r in range(dil):
                stage_ref[slot + c, pl.ds(r, t // dil, stride=dil), :] = ref[r, :, blk].astype(F32)
            cols.append(stage_ref[slot + c])
        return jnp.concatenate(cols, axis=1)

    o = [oc0_ref[...].astype(F32), token_order(oc1_ref, C_GROUPS[1][1], 0), token_order(oc2_ref, C_GROUPS[2][1], 2)]
    lse = [l0_ref[...], token_order(l1_ref, C_GROUPS[1][1], 4), token_order(l2_ref, C_GROUPS[2][1], 6)]
    mx = jnp.maximum(jnp.maximum(lse[0], lse[1]), lse[2])
    e = [jnp.exp(l - mx) for l in lse]
    den = e[0] + e[1] + e[2]
    oc = ((e[0] / den) * o[0] + (e[1] / den) * o[1] + (e[2] / den) * o[2]).astype(BF16)
    branches = (oa_ref[...], ob_ref[...], oc, od_ref[...])
    merged = None
    for bi in range(N_BRANCHES):
        term = jax.nn.sigmoid(_dot(h, wg_ref[bi])) * _dot(branches[bi], wb_ref[bi])
        merged = term if merged is None else merged + term
    xn = x_ref[...] + _dot(merged.astype(BF16), wo_ref[...])
    xo_ref[...] = xn
    hn = _rms(xn, gffn_ref[...])
    hi = hn.astype(BF16)
    h2_ref[...] = hi
    lo = (hn - hi.astype(F32)).astype(BF16)
    logits = _dot(hi, wrh_ref[...]) + _dot(lo, wrh_ref[...]) + _dot(hi, wrl_ref[...])
    logits = jnp.where(_lane_iota((1, LANE)) < N_EXPERTS, logits, NEG)
    ex = jnp.exp(logits - jnp.max(logits, axis=-1, keepdims=True))
    aff_ref[...] = ex / jnp.sum(ex, axis=-1, keepdims=True)


def _merge(x, h, oa, ob, oc, lse, od, seq_len, lw):
    n = x.shape[0]
    t = TILE_MERGE
    per_seq = seq_len // t
    row = lambda w: pl.BlockSpec((t, w), lambda i: (i, 0))
    planes = lambda dil: pl.BlockSpec((None, dil, t // dil, 256), lambda i: (i // per_seq, 0, i % per_seq, 0))
    p1, p2 = planes(C_GROUPS[1][1]), planes(C_GROUPS[2][1])
    return pl.pallas_call(
        _merge_kernel,
        grid=(n // t,),
        in_specs=[row(D_MODEL), row(D_MODEL), row(BRANCH_W), row(BRANCH_W), row(BRANCH_W), row(BRANCH_W),
                  p1, p1, p2, p2, row(BRANCH_W),
                  _const_spec((N_BRANCHES, D_MODEL, D_MODEL)), _const_spec((N_BRANCHES, BRANCH_W, D_MODEL)),
                  _const_spec((D_MODEL, D_MODEL)), _const_spec((1, D_MODEL)), _const_spec((D_MODEL, LANE)),
                  _const_spec((D_MODEL, LANE))],
        out_specs=[row(D_MODEL), row(D_MODEL), row(LANE)],
        out_shape=[jax.ShapeDtypeStruct((n, D_MODEL), F32), jax.ShapeDtypeStruct((n, D_MODEL), BF16),
                   jax.ShapeDtypeStruct((n, LANE), F32)],
        scratch_shapes=[pltpu.VMEM((8, t, LANE), F32)],
        compiler_params=_params("arbitrary"),
        name="gated_merge_router",
    )(x, h, oa, ob, oc[0], lse[0], oc[1], lse[1], oc[2], lse[2], od,
      lw["w_gate"], lw["w_branch"], lw["w_out"], lw["g_ffn"], lw["w_router_hi"], lw["w_router_lo"])


def _expert_kernel(xe_ref, gate_ref, wg_ref, wu_ref, wd_ref, ye_ref):
    xe = xe_ref[...]
    he = (jax.nn.silu(_dot(xe, wg_ref[...])) * _dot(xe, wu_ref[...])).astype(BF16)
    ye_ref[...] = _dot(he, wd_ref[...]) * gate_ref[...]


def _experts(xe, gate, lw):
    e, cap, d = xe.shape
    t = min(TILE_EXPERT, cap)
    ff = lw["w_exp_gate"].shape[-1]
    wspec = lambda a, b: pl.BlockSpec((None, a, b), lambda ei, ti: (ei, 0, 0))
    return pl.pallas_call(
        _expert_kernel,
        grid=(e, cap // t),
        in_specs=[pl.BlockSpec((None, t, d), lambda ei, ti: (ei, ti, 0)),
                  pl.BlockSpec((None, t, 1), lambda ei, ti: (ei, ti, 0)),
                  wspec(d, ff), wspec(d, ff), wspec(ff, d)],
        out_specs=pl.BlockSpec((None, t, d), lambda ei, ti: (ei, ti, 0)),
        out_shape=jax.ShapeDtypeStruct((e, cap, d), F32),
        compiler_params=_params("arbitrary", "arbitrary"),
        name="expert_swiglu",
    )(xe, gate, lw["w_exp_gate"], lw["w_exp_up"], lw["w_exp_down"])


def _combine_kernel(starts_ref, x_ref, tok_hbm, z_hbm, o_ref, tok_buf, z_buf, acc_ref, slot_ref, sem):
    b = pl.program_id(0)
    last = pl.num_programs(0) - 1
    t = x_ref.shape[0]
    w = COMBINE_WIN

    def span(blk):
        base = lax.shift_left(lax.shift_right_logical(starts_ref[blk], 7), 7)
        return base, lax.div(starts_ref[blk + 1] - base + (w - 1), w)

    def copies(row0, slot):
        row0 = pl.multiple_of(row0, LANE)
        return (pltpu.make_async_copy(tok_hbm.at[:, pl.ds(row0, w)], tok_buf.at[slot], sem.at[0, slot]),
                pltpu.make_async_copy(z_hbm.at[pl.ds(row0, w), :], z_buf.at[slot], sem.at[1, slot]))

    def start(row0, slot):
        for c in copies(row0, slot):
            c.start()

    def start_next_block(slot):
        @pl.when(b < last)
        def _():
            next_base, next_win = span(b + 1)

            @pl.when(next_win > 0)
            def _():
                start(next_base, slot)

    base, n_win = span(b)

    @pl.when(b == 0)
    def _():
        slot_ref[0] = 0

        @pl.when(n_win > 0)
        def _():
            start(base, 0)

    first = slot_ref[0]
    acc_ref[...] = jnp.zeros(acc_ref.shape, F32)
    ids = b * t + lax.broadcasted_iota(jnp.int32, (t, 1), 0)

    def window(i, carry):
        slot = lax.rem(first + i, 2)
        for c in copies(base + i * w, slot):
            c.wait()

        @pl.when(i + 1 < n_win)
        def _():
            start(base + (i + 1) * w, 1 - slot)

        @pl.when(i + 1 == n_win)
        def _():
            start_next_block(1 - slot)
            slot_ref[0] = 1 - slot

        onehot = jnp.where(tok_buf[slot] == ids, 1.0, 0.0).astype(BF16)
        z = z_buf[slot]
        hi = z.astype(BF16)
        lo = (z - hi.astype(F32)).astype(BF16)
        acc_ref[...] += _dot(onehot, hi) + _dot(onehot, lo)
        return carry

    lax.fori_loop(0, n_win, window, 0)

    @pl.when(n_win == 0)
    def _():
        start_next_block(first)

    o_ref[...] = x_ref[...] + acc_ref[...]


def _combine(x, tok_sorted, z, starts):
    n, d = x.shape
    t = TILE_COMBINE
    w = COMBINE_WIN
    return pl.pallas_call(
        _combine_kernel,
        grid_spec=pltpu.PrefetchScalarGridSpec(
            num_scalar_prefetch=1,
            grid=(n // t,),
            in_specs=[pl.BlockSpec((t, d), lambda i, s: (i, 0)), pl.BlockSpec(memory_space=pl.ANY),
                      pl.BlockSpec(memory_space=pl.ANY)],
            out_specs=pl.BlockSpec((t, d), lambda i, s: (i, 0)),
            scratch_shapes=[pltpu.VMEM((2, 1, w), jnp.int32), pltpu.VMEM((2, w, d), F32), pltpu.VMEM((t, d), F32),
                            pltpu.SMEM((1,), jnp.int32), pltpu.SemaphoreType.DMA((2, 2))],
        ),
        out_shape=jax.ShapeDtypeStruct((n, d), F32),
        compiler_params=_params("arbitrary"),
        name="expert_combine",
    )(starts, x, tok_sorted, z)


def _final_kernel(x_ref, g_ref, o_ref):
    o_ref[...] = _rms(x_ref[...], g_ref[...])


def _final_norm(x, g):
    n = x.shape[0]
    t = TILE_PROJ
    return pl.pallas_call(
        _final_kernel,
        grid=(n // t,),
        in_specs=[pl.BlockSpec((t, D_MODEL), lambda i: (i, 0)), _const_spec((1, D_MODEL))],
        out_specs=pl.BlockSpec((t, D_MODEL), lambda i: (i, 0)),
        out_shape=jax.ShapeDtypeStruct((n, D_MODEL), F32),
        compiler_params=_params("arbitrary"),
        name="final_norm",
    )(x, g)


def _t5_bucket_np(rel):
    nb = N_BUCKETS // 2
    max_exact = nb // 2
    n = np.abs(rel)
    large = max_exact + (np.log(np.maximum(n, 1).astype(np.float64) / max_exact)
                         / math.log(MAX_DISTANCE / max_exact) * (nb - max_exact)).astype(np.int32)
    large = np.minimum(large, nb - 1)
    return (np.where(rel > 0, nb, 0) + np.where(n < max_exact, n, large)).astype(np.int32)


def _band_buckets(rows, band, dil):
    rel = np.arange(rows + 2 * band, dtype=np.int32)[None, :] - band - np.arange(rows, dtype=np.int32)[:, None]
    return jnp.asarray(_t5_bucket_np(rel * dil))


def _position_tables(seq_len):
    t = jnp.arange(seq_len, dtype=jnp.int32)
    inv = ROPE_THETA ** (-jnp.arange(ROPE_HALF, dtype=F32) / ROPE_HALF)

    def cs(pos):
        ang = pos.astype(F32)[:, None] * inv[None, :]
        return jnp.cos(ang), jnp.sin(ang)

    def group(c, s):
        return jnp.concatenate([c, c], axis=1), jnp.concatenate([-s, s], axis=1)

    ones, zeros = jnp.ones((seq_len, 32), F32), jnp.zeros((seq_len, 32), F32)
    cb, sb = group(*cs(t))
    cr, sr = group(*cs(t // GRID_W))
    cc, sc = group(*cs(t % GRID_W))
    g64 = np.kron(np.eye(2, dtype=np.float32), np.full((HEAD_DIM, HEAD_DIM), 1.0 / HEAD_DIM, np.float32))
    pad_ones = np.tile(np.concatenate([np.zeros(64), np.ones(128), np.zeros(64)]).astype(np.float32), 2)
    return {
        "cos_b": jnp.concatenate([ones, ones, cb, ones], axis=1),
        "sin_b": jnp.concatenate([zeros, zeros, sb, zeros], axis=1),
        "cos_d": jnp.concatenate([cr, cc, cr, cc], axis=1),
        "sin_d": jnp.concatenate([sr, sc, sr, sc], axis=1),
        "g64": jnp.asarray(g64, BF16),
        "ones_b": jnp.asarray(pad_ones.reshape(1, 512)),
        "ones_d": jnp.asarray(pad_ones[:256].reshape(1, 256)),
    }


def _pad_value_heads(w):
    d, heads, _ = w.shape
    z = jnp.zeros((d, heads // 2, HEAD_DIM), w.dtype)
    return jnp.stack([w[:, 0::2], z, z, w[:, 1::2]], axis=2).reshape(d, heads * LANE)


def _pack_layer(l, p):
    w_in = p["w_in"][l]
    split = np.cumsum([0, 256, 128, 128, 256, 128, 32, 768, 768, 768, 256, 128, 128])
    qa, ka, va, cq, ckv, kr, qc, kc, vc, qd, kd, vd = [w_in[:, split[i]:split[i + 1]] for i in range(12)]

    def pair_heads(w):
        d = w.shape[0]
        return w.reshape(d, 4, HEAD_DIM)[:, jnp.array([0, 2, 1, 3])].reshape(d, 256)

    z = lambda w: jnp.zeros((w_in.shape[0], w), F32)
    packed = jnp.concatenate([pair_heads(qa), ka, va, cq, ckv, z(64), kr, z(32), qc, kc, vc, pair_heads(qd), kd,
                              _pad_value_heads(vd.reshape(-1, 2, HEAD_DIM))], axis=1)
    uq = p["mla_w_uq"][l].reshape(MLA_Q_RANK, MLA_HEADS, MLA_NOPE + MLA_ROPE)
    uq = jnp.pad(uq, ((0, 0), (0, 0), (0, LANE - MLA_NOPE - MLA_ROPE))).reshape(MLA_Q_RANK, MLA_HEADS * LANE)
    ukv = p["mla_w_ukv"][l].reshape(MLA_KV_RANK, MLA_HEADS, MLA_NOPE + MLA_V)
    ukv_k = jnp.pad(ukv[:, :, :MLA_NOPE], ((0, 0), (0, 0), (0, LANE - MLA_NOPE))).reshape(MLA_KV_RANK, MLA_HEADS * LANE)
    ukv_v = _pad_value_heads(ukv[:, :, MLA_NOPE:])
    wb = p["w_branch"][l]
    pair_rows = lambda w: w.reshape(4, HEAD_DIM, D_MODEL)[jnp.array([0, 2, 1, 3])].reshape(BRANCH_W, D_MODEL)
    wb = jnp.stack([pair_rows(wb[0]), wb[1], wb[2], pair_rows(wb[3])])
    wr = jnp.pad(p["w_router"][l], ((0, 0), (0, LANE - N_EXPERTS)))
    wr_hi = wr.astype(BF16)
    row = lambda g: g.reshape(1, -1).astype(F32)
    return {
        "g_mix": row(p["norm_mix"][l]), "w_in": packed.astype(BF16), "w_uq": uq.astype(BF16),
        "w_ukv_k": ukv_k.astype(BF16), "w_ukv_v": ukv_v.astype(BF16),
        "g_q": row(p["mla_q_norm"][l]), "g_kv": row(p["mla_kv_norm"][l]),
        "g_dq": row(jnp.tile(p["d_q_norm"][l], 2)), "g_dk": row(jnp.tile(p["d_k_norm"][l], 2)),
        "a_sink": p["a_sink"][l].astype(F32),
        "w_gate": p["w_gate"][l].astype(BF16), "w_branch": wb.astype(BF16), "w_out": p["w_out"][l].astype(BF16),
        "g_ffn": row(p["norm_ffn"][l]), "w_router_hi": wr_hi, "w_router_lo": (wr - wr_hi.astype(F32)).astype(BF16),
        "w_exp_gate": p["w_exp_gate"][l].astype(BF16), "w_exp_up": p["w_exp_up"][l].astype(BF16),
        "w_exp_down": p["w_exp_down"][l].astype(BF16),
    }


def _layer(x, b, l, lw, tabs, t5_table):
    n = b * l
    (h, qa, ka, va, qb, kb, vb, qc0, kc0, vc0, qc1, kc1, vc1, qc2, kc2, vc2, qd, kd, vd) = _proj(x, b, l, lw, tabs)
    seq = lambda a: a.reshape(b, l, a.shape[-1])
    flat = lambda a: a.reshape(n, a.shape[-1])

    oa = flat(_banded(seq(qa), seq(ka), seq(va), t5_table[:, :A_HEADS], tabs["bkt_a"], A_WINDOW, lw["a_sink"]))
    ob = flat(_flash_b(seq(qb), seq(kb), seq(vb)))
    od = flat(_flash_d(seq(qd), seq(kd), seq(vd)))

    oc, lse = [], []
    for g, (q, k, v) in enumerate(((seq(qc0), seq(kc0), seq(vc0)), (qc1, kc1, vc1), (qc2, kc2, vc2))):
        dil = C_GROUPS[g][1]
        table = t5_table[:, A_HEADS + g * C_HEADS:A_HEADS + (g + 1) * C_HEADS]
        streams = lambda a: a.reshape(b * dil, l // dil, 256)
        o_g, lse_g = _banded(streams(q), streams(k), streams(v), table, tabs["bkt_c"][g], C_BAND)
        if dil == 1:
            oc.append(flat(o_g))
            lse.append(flat(lse_g))
        else:
            oc.append(o_g.reshape(b, dil, l // dil, 256))
            lse.append(lse_g.reshape(b, dil, l // dil, 256))

    x_mid, h2, aff = _merge(x, h, oa, ob, oc, lse, od, l, lw)

    cap = EC_CAPACITY_FACTOR * n // N_EXPERTS
    gate, idx = lax.top_k(aff[:, :N_EXPERTS].T, cap)
    ye = _experts(h2[idx], gate[..., None], lw)

    pair_tok = idx.reshape(-1)
    order = jnp.argsort(pair_tok).astype(jnp.int32)
    pad = COMBINE_WIN + LANE
    tok_sorted = jnp.concatenate([pair_tok[order], jnp.full((pad,), -1, jnp.int32)])
    z = ye.reshape(-1, D_MODEL)[jnp.concatenate([order, jnp.zeros((pad,), jnp.int32)])]
    edges = jnp.arange(0, n + 1, TILE_COMBINE, dtype=jnp.int32)
    starts = jnp.sum((pair_tok[None, :] < edges[:, None]).astype(jnp.int32), axis=1)
    return _combine(x_mid, tok_sorted[None, :], z, starts)


def _trunk(x, layers, t5_table, norm_final):
    b, l, _ = x.shape
    tabs = _position_tables(l)
    tabs["bkt_a"] = _band_buckets(A_WINDOW, A_WINDOW, 1)
    tabs["bkt_c"] = [_band_buckets(2 * C_BAND, C_BAND, dil) for _, dil in C_GROUPS]
    x = x.reshape(b * l, D_MODEL)
    for lw in layers:
        x = _layer(x, b, l, lw, tabs, t5_table)
    return _final_norm(x, norm_final.reshape(1, -1)).reshape(b, l, D_MODEL)


def kernel(x_prompt, x_sample, t5_table, norm_mix, w_in, a_sink, mla_q_norm, mla_w_uq, mla_kv_norm, mla_w_ukv,
           d_q_norm, d_k_norm, w_gate, w_branch, w_out, norm_ffn, w_router, w_exp_gate, w_exp_up, w_exp_down,
           norm_final):
    p = dict(norm_mix=norm_mix, w_in=w_in, a_sink=a_sink, mla_q_norm=mla_q_norm, mla_w_uq=mla_w_uq,
             mla_kv_norm=mla_kv_norm, mla_w_ukv=mla_w_ukv, d_q_norm=d_q_norm, d_k_norm=d_k_norm, w_gate=w_gate,
             w_branch=w_branch, w_out=w_out, norm_ffn=norm_ffn, w_router=w_router, w_exp_gate=w_exp_gate,
             w_exp_up=w_exp_up, w_exp_down=w_exp_down)
    layers = [_pack_layer(l, p) for l in range(w_in.shape[0])]
    t5 = t5_table.astype(F32)
    return (_trunk(x_prompt, layers, t5, norm_final), _trunk(x_sample, layers, t5, norm_final))
```

```python
import functools
import math

import numpy as np
import jax
import jax.numpy as jnp
from jax import lax
from jax.experimental import pallas as pl
from jax.experimental.pallas import tpu as pltpu

F32 = jnp.float32
BF16 = jnp.bfloat16

D_MODEL = 1024
HEAD_DIM = 64
A_HEADS = 4
A_WINDOW = 128
MLA_HEADS = 4
MLA_Q_RANK = 256
MLA_KV_RANK = 128
MLA_NOPE = 64
MLA_ROPE = 32
MLA_V = 64
C_HEADS = 4
C_GROUPS = ((128, 1), (512, 4), (2048, 16))
C_BAND = 64
GRID_W = 64
N_BRANCHES = 4
BRANCH_W = 256
N_BUCKETS = 32
MAX_DISTANCE = 1024
N_EXPERTS = 16
EC_CAPACITY_FACTOR = 2
ROPE_THETA = 10000.0
ROPE_HALF = 16
EPS = 1e-6
NEG = -1e30

LANE = 128
VMEM_LIMIT = 56 * 1024 * 1024

SEG_QA, SEG_KA, SEG_VA = 0, 256, 384
SEG_CQ, SEG_CKV, SEG_KR = 512, 768, 896
SEG_QC, SEG_KC, SEG_VC = 1024, 1792, 2560
SEG_QD, SEG_KD, SEG_VD = 3328, 3584, 3712
IN_COLS_PACKED = 3968

LOG2E = math.log2(math.e)
A_SCALE = HEAD_DIM ** -0.5 * LOG2E
B_SCALE = (MLA_NOPE + MLA_ROPE) ** -0.5 * LOG2E

TILE_PROJ = 512
TILE_MERGE = 256
MERGE_COLS = 256
TILE_EXPERT = 512
TILE_COMBINE = 256
COMBINE_WIN = 768
BAND_TILE = 128
BAND_STEP = 512
FLASH_ROWS = 256
FLASH_KV = 512
FLASH_SUB = 32
FLASH_UNROLL = 8


def _dot(a, b):
    return jnp.dot(a, b, preferred_element_type=F32)


def _dot_nt(a, b):
    return lax.dot_general(a, b, (((1,), (1,)), ((), ())), preferred_element_type=F32)


def _params(*sem):
    return pltpu.CompilerParams(dimension_semantics=sem, vmem_limit_bytes=VMEM_LIMIT)


def _const_spec(shape):
    zeros = (0,) * len(shape)
    return pl.BlockSpec(shape, lambda *_: zeros, pipeline_mode=pl.Buffered(1))


def _smem_spec():
    return pl.BlockSpec(memory_space=pltpu.SMEM)


def _lane_iota(shape):
    return lax.broadcasted_iota(jnp.int32, shape, len(shape) - 1)


def _rope128(x, cos, sin_signed, first_half):
    rot = jnp.where(first_half, pltpu.roll(x, LANE - ROPE_HALF, 1), pltpu.roll(x, ROPE_HALF, 1))
    return x * cos + rot * sin_signed


def _rms(x, gain):
    return x * lax.rsqrt(jnp.mean(x * x, axis=-1, keepdims=True) + EPS) * gain


def _lane_blocks(width):
    return [slice(c * LANE, (c + 1) * LANE) for c in range(width // LANE)]


def _proj_kernel(x_ref, gmix_ref, win_ref, wuq_ref, wukvk_ref, wukvv_ref, gq_ref, gkv_ref, gdq_ref, gdk_ref,
                 g64_ref, oneb_ref, oned_ref, cosb_ref, sinb_ref, cosd_ref, sind_ref,
                 h_ref, qa_ref, ka_ref, va_ref, qb_ref, kb_ref, vb_ref,
                 qc0_ref, kc0_ref, vc0_ref, qc1_ref, kc1_ref, vc1_ref, qc2_ref, kc2_ref, vc2_ref,
                 qd_ref, kd_ref, vd_ref, stage_ref):
    h = _rms(x_ref[...], gmix_ref[...]).astype(BF16)
    h_ref[...] = h

    def seg(lo, width):
        return _dot(h, win_ref[:, lo:lo + width])

    first_half = jnp.bitwise_and(_lane_iota((1, LANE)), 2 * ROPE_HALF - 1) < ROPE_HALF

    qa_ref[...] = (seg(SEG_QA, 256) * A_SCALE).astype(BF16)
    ka_ref[...] = seg(SEG_KA, 128).astype(BF16)
    va_ref[...] = seg(SEG_VA, 128).astype(BF16)

    cosb, sinb = cosb_ref[...], sinb_ref[...]
    cqn = _rms(seg(SEG_CQ, MLA_Q_RANK), gq_ref[...]).astype(BF16)
    qm = _dot(cqn, wuq_ref[...])
    for blk in _lane_blocks(MLA_HEADS * LANE):
        qb_ref[:, blk] = (_rope128(qm[:, blk], cosb, sinb, first_half) * B_SCALE).astype(BF16)
    ckvn = _rms(seg(SEG_CKV, MLA_KV_RANK), gkv_ref[...]).astype(BF16)
    k_pe = _rope128(seg(SEG_KR, LANE), cosb, sinb, first_half)
    kn = _dot(ckvn, wukvk_ref[...])
    for blk in _lane_blocks(MLA_HEADS * LANE):
        kb_ref[:, blk] = (kn[:, blk] + k_pe).astype(BF16)
    vb_ref[...] = (_dot(ckvn, wukvv_ref[...]) + oneb_ref[...]).astype(BF16)

    t = x_ref.shape[0]
    c_out = ((qc0_ref, kc0_ref, vc0_ref), (qc1_ref, kc1_ref, vc1_ref), (qc2_ref, kc2_ref, vc2_ref))
    for a, (lo, scale) in enumerate(((SEG_QC, A_SCALE), (SEG_KC, None), (SEG_VC, None))):
        z = seg(lo, 768)
        if scale is not None:
            z = z * scale
        c_out[0][a][...] = z[:, :256].astype(BF16)
        for g in (1, 2):
            dil = C_GROUPS[g][1]
            for c, blk in enumerate(_lane_blocks(256)):
                slot = (a * 2 + (g - 1)) * 2 + c
                stage_ref[slot] = z[:, g * 256 + c * LANE:g * 256 + (c + 1) * LANE]
                for r in range(dil):
                    c_out[g][a][r, :, blk] = stage_ref[slot, pl.ds(r, t // dil, stride=dil), :].astype(BF16)

    cosd, sind = cosd_ref[...], sind_ref[...]
    g64 = g64_ref[...]

    def head_norm(z, gain):
        sq = z * z
        hi = sq.astype(BF16)
        lo = (sq - hi.astype(F32)).astype(BF16)
        ms = _dot(jnp.concatenate([hi, lo], axis=1), g64)
        return z * lax.rsqrt(ms + EPS) * gain

    zq = seg(SEG_QD, 256)
    for blk in _lane_blocks(256):
        y = head_norm(zq[:, blk], gdq_ref[...])
        qd_ref[:, blk] = (_rope128(y, cosd, sind, first_half) * A_SCALE).astype(BF16)
    kd_ref[...] = _rope128(head_norm(seg(SEG_KD, 128), gdk_ref[...]), cosd, sind, first_half).astype(BF16)
    vd_ref[...] = (seg(SEG_VD, 256) + oned_ref[...]).astype(BF16)


def _proj(x, batch, seq_len, lw, tabs):
    n = x.shape[0]
    t = TILE_PROJ
    per_seq = seq_len // t
    row = lambda w: pl.BlockSpec((t, w), lambda i: (i, 0))
    tab = pl.BlockSpec((t, LANE), lambda i: (i % per_seq, 0))
    flat = lambda w: (row(w), jax.ShapeDtypeStruct((n, w), BF16))

    def planes(dil):
        return (pl.BlockSpec((None, dil, t // dil, 256), lambda i: (i // per_seq, 0, i % per_seq, 0)),
                jax.ShapeDtypeStruct((batch, dil, seq_len // dil, 256), BF16))

    outs = ([flat(D_MODEL), flat(256), flat(128), flat(128), flat(512), flat(512), flat(512)]
            + [flat(256)] * 3 + [planes(C_GROUPS[1][1])] * 3 + [planes(C_GROUPS[2][1])] * 3
            + [flat(256), flat(128), flat(256)])
    return pl.pallas_call(
        _proj_kernel,
        grid=(n // t,),
        in_specs=[row(D_MODEL), _const_spec((1, D_MODEL)), _const_spec((D_MODEL, IN_COLS_PACKED)),
                  _const_spec((MLA_Q_RANK, 512)), _const_spec((MLA_KV_RANK, 512)), _const_spec((MLA_KV_RANK, 512)),
                  _const_spec((1, MLA_Q_RANK)), _const_spec((1, MLA_KV_RANK)), _const_spec((1, LANE)),
                  _const_spec((1, LANE)), _const_spec((2 * LANE, LANE)), _const_spec((1, 512)), _const_spec((1, 256)),
                  tab, tab, tab, tab],
        out_specs=[o[0] for o in outs],
        out_shape=[o[1] for o in outs],
        scratch_shapes=[pltpu.VMEM((12, t, LANE), F32)],
        compiler_params=_params("arbitrary"),
        name="norm_in_proj",
    )(x, lw["g_mix"], lw["w_in"], lw["w_uq"], lw["w_ukv_k"], lw["w_ukv_v"], lw["g_q"], lw["g_kv"], lw["g_dq"],
      lw["g_dk"], tabs["g64"], tabs["ones_b"], tabs["ones_d"], tabs["cos_b"], tabs["sin_b"], tabs["cos_d"],
      tabs["sin_d"])


def _build_bias(tab_ref, bkt_ref, bias_ref, n_heads, band):
    bkt = bkt_ref[...]
    row = lax.broadcasted_iota(jnp.int32, bkt.shape, 0)
    col = lax.broadcasted_iota(jnp.int32, bkt.shape, 1)
    valid = jnp.abs(col - band - row) <= band
    for hh in range(n_heads):
        def body(bb, acc, hh=hh):
            return jnp.where(bkt == bb, tab_ref[bb, hh], acc)
        acc = lax.fori_loop(0, N_BUCKETS, body, jnp.zeros(bkt.shape, F32))
        bias_ref[hh] = jnp.where(valid, acc * LOG2E, NEG)


def _banded_kernel(*refs, band, n_heads, paired_heads, has_sink):
    refs = list(refs)
    tab_ref = refs.pop(0)
    sink_ref = refs.pop(0) if has_sink else None
    bkt_ref, q_ref, kl_ref, kc_ref, kr_ref, vl_ref, vc_ref, vr_ref, o_ref = refs[:9]
    lse_ref = None if has_sink else refs[9]
    bias_ref, s_ref = refs[-2:]
    n = pl.program_id(1)
    last = pl.num_programs(1) - 1

    @pl.when((pl.program_id(0) == 0) & (n == 0))
    def _():
        _build_bias(tab_ref, bkt_ref, bias_ref, n_heads, band)

    t = BAND_TILE
    win = t + 2 * band
    n_sub = q_ref.shape[0] // t
    col = _lane_iota((t, win))
    is_lo = _lane_iota((1, LANE)) < HEAD_DIM
    zero = jnp.zeros((t, LANE), BF16)
    for blk in range(2):
        lanes = slice(blk * LANE, (blk + 1) * LANE)
        kv_lanes = slice(0, LANE) if paired_heads else lanes
        kall = jnp.concatenate([kl_ref[:, kv_lanes], kc_ref[:, kv_lanes], kr_ref[:, kv_lanes]], axis=0)
        for j in range(n_sub):
            qb = q_ref[j * t:(j + 1) * t, lanes]
            for half in range(2):
                qm = jnp.where(is_lo if half == 0 else jnp.logical_not(is_lo), qb, zero)
                s_ref[(blk * n_sub + j) * 2 + half] = _dot_nt(qm, kall[j * t:j * t + win])
    for blk in range(2):
        lanes = slice(blk * LANE, (blk + 1) * LANE)
        kv_lanes = slice(0, LANE) if paired_heads else lanes
        vall = jnp.concatenate([vl_ref[:, kv_lanes], vc_ref[:, kv_lanes], vr_ref[:, kv_lanes]], axis=0)
        for j in range(n_sub):
            rows = slice(j * t, (j + 1) * t)
            vwin = vall[j * t:j * t + win]
            outs, lses = [], []
            for half in range(2):
                head = blk + 2 * half if paired_heads else 2 * blk + half
                s = s_ref[(blk * n_sub + j) * 2 + half] + bias_ref[head]
                if j == 0:
                    s = jnp.where(col >= jnp.where(n > 0, 0, band), s, NEG)
                if j == n_sub - 1:
                    s = jnp.where(col < jnp.where(n < last, win, t + band), s, NEG)
                m = jnp.max(s, axis=-1, keepdims=True)
                if has_sink:
                    sk = sink_ref[head] * LOG2E
                    m = jnp.maximum(m, sk)
                p = jnp.exp2(s - m)
                denom = jnp.sum(p, axis=-1, keepdims=True)
                if has_sink:
                    denom = denom + jnp.exp2(sk - m)
                outs.append(_dot(p.astype(BF16), vwin) / denom)
                if lse_ref is not None:
                    lses.append(m * (1.0 / LOG2E) + jnp.log(denom))
            o_ref[rows, lanes] = jnp.where(is_lo, outs[0], outs[1]).astype(BF16)
            if lse_ref is not None:
                lse_ref[rows, lanes] = jnp.where(is_lo, lses[0], lses[1])


def _banded(q, k, v, table, bkt, band, sink=None):
    s, m, _ = q.shape
    kv_w = k.shape[-1]
    rows = min(BAND_STEP, m)
    per = rows // band
    nh = m // band
    halo = lambda f: pl.BlockSpec((None, band, kv_w), lambda si, ni: (si, f(ni), 0))
    left, right = (lambda ni: jnp.maximum(per * ni - 1, 0)), (lambda ni: jnp.minimum(per * (ni + 1), nh - 1))
    mid = lambda w: pl.BlockSpec((None, rows, w), lambda si, ni: (si, ni, 0))
    win = BAND_TILE + 2 * band
    has_sink = sink is not None
    out_bf = jax.ShapeDtypeStruct((s, m, 256), BF16)
    return pl.pallas_call(
        functools.partial(_banded_kernel, band=band, n_heads=table.shape[1], paired_heads=has_sink,
                          has_sink=has_sink),
        grid=(s, m // rows),
        in_specs=[_smem_spec()] + ([_smem_spec()] if has_sink else [])
                 + [_const_spec((BAND_TILE, win)), mid(256), halo(left), mid(kv_w), halo(right), halo(left),
                    mid(kv_w), halo(right)],
        out_specs=mid(256) if has_sink else [mid(256), mid(256)],
        out_shape=out_bf if has_sink else [out_bf, jax.ShapeDtypeStruct((s, m, 256), F32)],
        scratch_shapes=[pltpu.VMEM((table.shape[1], BAND_TILE, win), F32),
                        pltpu.VMEM((4 * (rows // BAND_TILE), BAND_TILE, win), F32)],
        compiler_params=_params("arbitrary", "arbitrary"),
        name="attn_window_sink" if has_sink else "attn_dilated",
    )(*([table] + ([sink] if has_sink else []) + [bkt, q, k, k, k, v, v, v]))


def _flash_chains(qs, k_refs, v_refs, s_ref, p_ref, m_ref, a_ref, acc_ref):
    nc = len(qs)
    rows = qs[0].shape[0]
    n_chunks = k_refs[0].shape[0] // FLASH_KV
    unroll = min(FLASH_UNROLL, n_chunks)
    ncol = FLASH_KV // LANE

    def scores(slot, chunk):
        start = pl.multiple_of(chunk * FLASH_KV, FLASH_KV)
        for c in range(nc):
            s_ref[slot, c] = _dot_nt(qs[c], k_refs[c][pl.ds(start, FLASH_KV), :])

    def consume(slot, chunk):
        start = pl.multiple_of(chunk * FLASH_KV, FLASH_KV)
        for c in range(nc):
            for r0 in range(0, rows, FLASH_SUB):
                rs = slice(r0, r0 + FLASH_SUB)
                sb = [s_ref[slot, c, rs, j * LANE:(j + 1) * LANE] for j in range(ncol)]
                smax = functools.reduce(jnp.maximum, sb)
                m_old = m_ref[c, rs, :]
                m_new = jnp.maximum(m_old, jnp.max(smax, axis=-1, keepdims=True))
                m_ref[c, rs, :] = m_new
                a_ref[c, rs, :] = jnp.exp2(m_old - m_new)
                for j in range(ncol):
                    p_ref[c, rs, j * LANE:(j + 1) * LANE] = jnp.exp2(sb[j] - m_new).astype(BF16)
            acc_ref[c] = a_ref[c] * acc_ref[c] + _dot(p_ref[c], v_refs[c][pl.ds(start, FLASH_KV), :])

    m_ref[...] = jnp.full(m_ref.shape, NEG, F32)
    acc_ref[...] = jnp.zeros(acc_ref.shape, F32)
    scores(0, 0)

    def trip(i, carry):
        base = i * unroll
        for u in range(unroll):
            scores((u + 1) % 2, base + u + 1)
            consume(u % 2, base + u)
        return carry

    lax.fori_loop(0, n_chunks // unroll - 1, trip, 0)
    base = n_chunks - unroll
    for u in range(unroll):
        if u + 1 < unroll:
            scores((u + 1) % 2, base + u + 1)
        consume(u % 2, base + u)
    return [acc_ref[c] / pltpu.roll(acc_ref[c], HEAD_DIM, 1) for c in range(nc)]


def _flash_scratch(nc, rows):
    stat = pltpu.VMEM((nc, rows, LANE), F32)
    return [pltpu.VMEM((2, nc, rows, FLASH_KV), F32), pltpu.VMEM((nc, rows, FLASH_KV), BF16), stat, stat, stat]


def _flash_b_kernel(qlo_ref, qhi_ref, klo_ref, khi_ref, vlo_ref, vhi_ref, o_ref, *scratch):
    o_lo, o_hi = _flash_chains([qlo_ref[...], qhi_ref[...]], [klo_ref, khi_ref], [vlo_ref, vhi_ref], *scratch)
    is_lo = _lane_iota((1, LANE)) < HEAD_DIM
    o_ref[...] = jnp.where(is_lo, o_lo, o_hi).astype(BF16)


def _flash_b(q, k, v):
    b, l, _ = q.shape
    tq = FLASH_ROWS
    qs = lambda off: pl.BlockSpec((None, tq, LANE), lambda bi, pi, qi: (bi, qi, 2 * pi + off))
    ks = lambda off: pl.BlockSpec((None, l, LANE), lambda bi, pi, qi: (bi, 0, 2 * pi + off))
    return pl.pallas_call(
        _flash_b_kernel,
        grid=(b, 2, l // tq),
        in_specs=[qs(0), qs(1), ks(0), ks(1), ks(0), ks(1)],
        out_specs=pl.BlockSpec((None, tq, LANE), lambda bi, pi, qi: (bi, qi, pi)),
        out_shape=jax.ShapeDtypeStruct((b, l, 256), BF16),
        scratch_shapes=_flash_scratch(2, FLASH_ROWS),
        compiler_params=_params("arbitrary", "arbitrary", "arbitrary"),
        name="attn_dense_mla",
    )(q, q, k, k, v, v)


def _flash_d_kernel(q_ref, k_ref, vlo_ref, vhi_ref, o_ref, *scratch):
    q = q_ref[...]
    tq = q.shape[0]
    is_lo = _lane_iota((1, LANE)) < HEAD_DIM
    zero = jnp.zeros((tq, LANE), BF16)
    qb = [q[:, :LANE], q[:, LANE:]]
    qs = [jnp.concatenate([jnp.where(keep, qb[0], zero), jnp.where(keep, qb[1], zero)], axis=0)
          for keep in (is_lo, jnp.logical_not(is_lo))]
    o_lo, o_hi = _flash_chains(qs, [k_ref, k_ref], [vlo_ref, vhi_ref], *scratch)
    o_ref[:, :LANE] = jnp.where(is_lo, o_lo[:tq], o_hi[:tq]).astype(BF16)
    o_ref[:, LANE:] = jnp.where(is_lo, o_lo[tq:], o_hi[tq:]).astype(BF16)


def _flash_d(q, k, v):
    b, l, _ = q.shape
    tq = FLASH_ROWS // 2
    whole = lambda blk: pl.BlockSpec((None, l, LANE), lambda bi, qi: (bi, 0, blk))
    tile = pl.BlockSpec((None, tq, 256), lambda bi, qi: (bi, qi, 0))
    return pl.pallas_call(
        _flash_d_kernel,
        grid=(b, l // tq),
        in_specs=[tile, whole(0), whole(0), whole(1)],
        out_specs=tile,
        out_shape=jax.ShapeDtypeStruct((b, l, 256), BF16),
        scratch_shapes=_flash_scratch(2, FLASH_ROWS),
        compiler_params=_params("arbitrary", "arbitrary"),
        name="attn_dense_axial",
    )(q, k, v, v)


def _merge_kernel(x_ref, h_ref, oa_ref, ob_ref, oc0_ref, l0_ref, oc1_ref, l1_ref, oc2_ref, l2_ref, od_ref,
                  wg_ref, wb_ref, wo_ref, gffn_ref, wr_ref, xo_ref, h2_ref, aff_ref, stage_ref, merged_ref):
    h = h_ref[...]
    t = h.shape[0]

    def token_order(ref, dil, slot):
        cols = []
        for c, blk in enumerate(_lane_blocks(256)):
            for r in range(dil):
                stage_ref[slot + c, pl.ds(r, t // dil, stride=dil), :] = ref[r, :, blk].astype(F32)
            cols.append(stage_ref[slot + c])
        return jnp.concatenate(cols, axis=1)

    o = [oc0_ref[...].astype(F32), token_order(oc1_ref, C_GROUPS[1][1], 0), token_order(oc2_ref, C_GROUPS[2][1], 2)]
    lse = [l0_ref[...], token_order(l1_ref, C_GROUPS[1][1], 4), token_order(l2_ref, C_GROUPS[2][1], 6)]
    mx = jnp.maximum(jnp.maximum(lse[0], lse[1]), lse[2])
    e = [jnp.exp(l - mx) for l in lse]
    den = e[0] + e[1] + e[2]
    oc = ((e[0] / den) * o[0] + (e[1] / den) * o[1] + (e[2] / den) * o[2]).astype(BF16)
    branches = (oa_ref[...], ob_ref[...], oc, od_ref[...])
    for cols in (slice(c, c + MERGE_COLS) for c in range(0, D_MODEL, MERGE_COLS)):
        merged = None
        for bi in (0, 1, 3, 2):
            term = jax.nn.sigmoid(_dot(h, wg_ref[bi, :, cols])) * _dot(branches[bi], wb_ref[bi, :, cols])
            merged = term if merged is None else merged + term
        merged_ref[:, cols] = merged.astype(BF16)
    xn = x_ref[...] + _dot(merged_ref[...], wo_ref[...])
    xo_ref[...] = xn
    hn = _rms(xn, gffn_ref[...]).astype(BF16)
    h2_ref[...] = hn
    logits = jnp.where(_lane_iota((1, LANE)) < N_EXPERTS, _dot(hn, wr_ref[...]), NEG)
    ex = jnp.exp(logits - jnp.max(logits, axis=-1, keepdims=True))
    aff_ref[...] = ex / jnp.sum(ex, axis=-1, keepdims=True)


def _merge(x, h, oa, ob, oc, lse, od, seq_len, lw):
    n = x.shape[0]
    t = TILE_MERGE
    per_seq = seq_len // t
    row = lambda w: pl.BlockSpec((t, w), lambda i: (i, 0))
    planes = lambda dil: pl.BlockSpec((None, dil, t // dil, 256), lambda i: (i // per_seq, 0, i % per_seq, 0))
    p1, p2 = planes(C_GROUPS[1][1]), planes(C_GROUPS[2][1])
    return pl.pallas_call(
        _merge_kernel,
        grid=(n // t,),
        in_specs=[row(D_MODEL), row(D_MODEL), row(BRANCH_W), row(BRANCH_W), row(BRANCH_W), row(BRANCH_W),
                  p1, p1, p2, p2, row(BRANCH_W),
                  _const_spec((N_BRANCHES, D_MODEL, D_MODEL)), _const_spec((N_BRANCHES, BRANCH_W, D_MODEL)),
                  _const_spec((D_MODEL, D_MODEL)), _const_spec((1, D_MODEL)), _const_spec((D_MODEL, LANE))],
        out_specs=[row(D_MODEL), row(D_MODEL), row(LANE)],
        out_shape=[jax.ShapeDtypeStruct((n, D_MODEL), F32), jax.ShapeDtypeStruct((n, D_MODEL), BF16),
                   jax.ShapeDtypeStruct((n, LANE), F32)],
        scratch_shapes=[pltpu.VMEM((8, t, LANE), F32), pltpu.VMEM((t, D_MODEL), BF16)],
        compiler_params=_params("arbitrary"),
        name="gated_merge_router",
    )(x, h, oa, ob, oc[0], lse[0], oc[1], lse[1], oc[2], lse[2], od,
      lw["w_gate"], lw["w_branch"], lw["w_out"], lw["g_ffn"], lw["w_router"])


def _expert_kernel(xe_ref, gate_ref, wg_ref, wu_ref, wd_ref, ye_ref):
    xe = xe_ref[...]
    he = (jax.nn.silu(_dot(xe, wg_ref[...])) * _dot(xe, wu_ref[...])).astype(BF16)
    ye_ref[...] = _dot(he, wd_ref[...]) * gate_ref[...]


def _experts(xe, gate, lw):
    e, cap, d = xe.shape
    t = min(TILE_EXPERT, cap)
    ff = lw["w_exp_gate"].shape[-1]
    wspec = lambda a, b: pl.BlockSpec((None, a, b), lambda ei, ti: (ei, 0, 0))
    return pl.pallas_call(
        _expert_kernel,
        grid=(e, cap // t),
        in_specs=[pl.BlockSpec((None, t, d), lambda ei, ti: (ei, ti, 0)),
                  pl.BlockSpec((None, t, 1), lambda ei, ti: (ei, ti, 0)),
                  wspec(d, ff), wspec(d, ff), wspec(ff, d)],
        out_specs=pl.BlockSpec((None, t, d), lambda ei, ti: (ei, ti, 0)),
        out_shape=jax.ShapeDtypeStruct((e, cap, d), F32),
        compiler_params=_params("arbitrary", "arbitrary"),
        name="expert_swiglu",
    )(xe, gate, lw["w_exp_gate"], lw["w_exp_up"], lw["w_exp_down"])


def _combine_kernel(starts_ref, x_ref, tok_hbm, z_hbm, *refs, final_norm):
    gain_ref = refs[0] if final_norm else None
    o_ref, tok_buf, z_buf, acc_ref, slot_ref, sem = refs[-6:]
    b = pl.program_id(0)
    last = pl.num_programs(0) - 1
    t = x_ref.shape[0]
    w = COMBINE_WIN

    def span(blk):
        base = lax.shift_left(lax.shift_right_logical(starts_ref[blk], 7), 7)
        return base, lax.div(starts_ref[blk + 1] - base + (w - 1), w)

    def copies(row0, slot):
        row0 = pl.multiple_of(row0, LANE)
        return (pltpu.make_async_copy(tok_hbm.at[:, pl.ds(row0, w)], tok_buf.at[slot], sem.at[0, slot]),
                pltpu.make_async_copy(z_hbm.at[pl.ds(row0, w), :], z_buf.at[slot], sem.at[1, slot]))

    def start(row0, slot):
        for c in copies(row0, slot):
            c.start()

    def start_next_block(slot):
        @pl.when(b < last)
        def _():
            next_base, next_win = span(b + 1)

            @pl.when(next_win > 0)
            def _():
                start(next_base, slot)

    base, n_win = span(b)

    @pl.when(b == 0)
    def _():
        slot_ref[0] = 0

        @pl.when(n_win > 0)
        def _():
            start(base, 0)

    first = slot_ref[0]
    acc_ref[...] = jnp.zeros(acc_ref.shape, F32)
    ids = b * t + lax.broadcasted_iota(jnp.int32, (t, 1), 0)

    def window(i, carry):
        slot = lax.rem(first + i, 2)
        for c in copies(base + i * w, slot):
            c.wait()

        @pl.when(i + 1 < n_win)
        def _():
            start(base + (i + 1) * w, 1 - slot)

        @pl.when(i + 1 == n_win)
        def _():
            start_next_block(1 - slot)
            slot_ref[0] = 1 - slot

        onehot = jnp.where(tok_buf[slot] == ids, 1.0, 0.0).astype(BF16)
        z = z_buf[slot]
        hi = z.astype(BF16)
        lo = (z - hi.astype(F32)).astype(BF16)
        acc_ref[...] += _dot(onehot, hi) + _dot(onehot, lo)
        return carry

    lax.fori_loop(0, n_win, window, 0)

    @pl.when(n_win == 0)
    def _():
        start_next_block(first)

    out = x_ref[...] + acc_ref[...]
    o_ref[...] = _rms(out, gain_ref[...]) if final_norm else out


def _combine(x, tok_sorted, z, starts, final_gain=None):
    n, d = x.shape
    t = TILE_COMBINE
    w = COMBINE_WIN
    final_norm = final_gain is not None
    return pl.pallas_call(
        functools.partial(_combine_kernel, final_norm=final_norm),
        grid_spec=pltpu.PrefetchScalarGridSpec(
            num_scalar_prefetch=1,
            grid=(n // t,),
            in_specs=[pl.BlockSpec((t, d), lambda i, s: (i, 0)), pl.BlockSpec(memory_space=pl.ANY),
                      pl.BlockSpec(memory_space=pl.ANY)]
                     + ([pl.BlockSpec((1, d), lambda i, s: (0, 0))] if final_norm else []),
            out_specs=pl.BlockSpec((t, d), lambda i, s: (i, 0)),
            scratch_shapes=[pltpu.VMEM((2, 1, w), jnp.int32), pltpu.VMEM((2, w, d), F32), pltpu.VMEM((t, d), F32),
                            pltpu.SMEM((1,), jnp.int32), pltpu.SemaphoreType.DMA((2, 2))],
        ),
        out_shape=jax.ShapeDtypeStruct((n, d), F32),
        compiler_params=_params("arbitrary"),
        name="expert_combine",
    )(*([starts, x, tok_sorted, z] + ([final_gain] if final_norm else [])))


def _t5_bucket_np(rel):
    nb = N_BUCKETS // 2
    max_exact = nb // 2
    n = np.abs(rel)
    large = max_exact + (np.log(np.maximum(n, 1).astype(np.float64) / max_exact)
                         / math.log(MAX_DISTANCE / max_exact) * (nb - max_exact)).astype(np.int32)
    large = np.minimum(large, nb - 1)
    return (np.where(rel > 0, nb, 0) + np.where(n < max_exact, n, large)).astype(np.int32)


def _band_buckets(rows, band, dil):
    rel = np.arange(rows + 2 * band, dtype=np.int32)[None, :] - band - np.arange(rows, dtype=np.int32)[:, None]
    return jnp.asarray(_t5_bucket_np(rel * dil))


def _position_tables(seq_len):
    t = jnp.arange(seq_len, dtype=jnp.int32)
    inv = ROPE_THETA ** (-jnp.arange(ROPE_HALF, dtype=F32) / ROPE_HALF)

    def cs(pos):
        ang = pos.astype(F32)[:, None] * inv[None, :]
        return jnp.cos(ang), jnp.sin(ang)

    def group(c, s):
        return jnp.concatenate([c, c], axis=1), jnp.concatenate([-s, s], axis=1)

    ones, zeros = jnp.ones((seq_len, 32), F32), jnp.zeros((seq_len, 32), F32)
    cb, sb = group(*cs(t))
    cr, sr = group(*cs(t // GRID_W))
    cc, sc = group(*cs(t % GRID_W))
    g64 = np.kron(np.eye(2, dtype=np.float32), np.full((HEAD_DIM, HEAD_DIM), 1.0 / HEAD_DIM, np.float32))
    pad_ones = np.tile(np.concatenate([np.zeros(64), np.ones(128), np.zeros(64)]).astype(np.float32), 2)
    return {
        "cos_b": jnp.concatenate([ones, ones, cb, ones], axis=1),
        "sin_b": jnp.concatenate([zeros, zeros, sb, zeros], axis=1),
        "cos_d": jnp.concatenate([cr, cc, cr, cc], axis=1),
        "sin_d": jnp.concatenate([sr, sc, sr, sc], axis=1),
        "g64": jnp.asarray(np.concatenate([g64, g64], axis=0), BF16),
        "ones_b": jnp.asarray(pad_ones.reshape(1, 512)),
        "ones_d": jnp.asarray(pad_ones[:256].reshape(1, 256)),
    }


def _pad_value_heads(w):
    d, heads, _ = w.shape
    z = jnp.zeros((d, heads // 2, HEAD_DIM), w.dtype)
    return jnp.stack([w[:, 0::2], z, z, w[:, 1::2]], axis=2).reshape(d, heads * LANE)


def _pack_layer(l, p):
    w_in = p["w_in"][l]
    split = np.cumsum([0, 256, 128, 128, 256, 128, 32, 768, 768, 768, 256, 128, 128])
    qa, ka, va, cq, ckv, kr, qc, kc, vc, qd, kd, vd = [w_in[:, split[i]:split[i + 1]] for i in range(12)]

    def pair_heads(w):
        d = w.shape[0]
        return w.reshape(d, 4, HEAD_DIM)[:, jnp.array([0, 2, 1, 3])].reshape(d, 256)

    z = lambda w: jnp.zeros((w_in.shape[0], w), F32)
    packed = jnp.concatenate([pair_heads(qa), ka, va, cq, ckv, z(64), kr, z(32), qc, kc, vc, pair_heads(qd), kd,
                              _pad_value_heads(vd.reshape(-1, 2, HEAD_DIM))], axis=1)
    uq = p["mla_w_uq"][l].reshape(MLA_Q_RANK, MLA_HEADS, MLA_NOPE + MLA_ROPE)
    uq = jnp.pad(uq, ((0, 0), (0, 0), (0, LANE - MLA_NOPE - MLA_ROPE))).reshape(MLA_Q_RANK, MLA_HEADS * LANE)
    ukv = p["mla_w_ukv"][l].reshape(MLA_KV_RANK, MLA_HEADS, MLA_NOPE + MLA_V)
    ukv_k = jnp.pad(ukv[:, :, :MLA_NOPE], ((0, 0), (0, 0), (0, LANE - MLA_NOPE))).reshape(MLA_KV_RANK, MLA_HEADS * LANE)
    ukv_v = _pad_value_heads(ukv[:, :, MLA_NOPE:])
    wb = p["w_branch"][l]
    pair_rows = lambda w: w.reshape(4, HEAD_DIM, D_MODEL)[jnp.array([0, 2, 1, 3])].reshape(BRANCH_W, D_MODEL)
    wb = jnp.stack([pair_rows(wb[0]), wb[1], wb[2], pair_rows(wb[3])])
    wr = jnp.pad(p["w_router"][l], ((0, 0), (0, LANE - N_EXPERTS)))
    row = lambda g: g.reshape(1, -1).astype(F32)
    return {
        "g_mix": row(p["norm_mix"][l]), "w_in": packed.astype(BF16), "w_uq": uq.astype(BF16),
        "w_ukv_k": ukv_k.astype(BF16), "w_ukv_v": ukv_v.astype(BF16),
        "g_q": row(p["mla_q_norm"][l]), "g_kv": row(p["mla_kv_norm"][l]),
        "g_dq": row(jnp.tile(p["d_q_norm"][l], 2)), "g_dk": row(jnp.tile(p["d_k_norm"][l], 2)),
        "a_sink": p["a_sink"][l].astype(F32),
        "w_gate": p["w_gate"][l].astype(BF16), "w_branch": wb.astype(BF16), "w_out": p["w_out"][l].astype(BF16),
        "g_ffn": row(p["norm_ffn"][l]), "w_router": wr.astype(BF16),
        "w_exp_gate": p["w_exp_gate"][l].astype(BF16), "w_exp_up": p["w_exp_up"][l].astype(BF16),
        "w_exp_down": p["w_exp_down"][l].astype(BF16),
    }


def _layer(x, b, l, lw, tabs, t5_table, final_gain=None):
    n = b * l
    (h, qa, ka, va, qb, kb, vb, qc0, kc0, vc0, qc1, kc1, vc1, qc2, kc2, vc2, qd, kd, vd) = _proj(x, b, l, lw, tabs)
    seq = lambda a: a.reshape(b, l, a.shape[-1])
    flat = lambda a: a.reshape(n, a.shape[-1])

    oa = flat(_banded(seq(qa), seq(ka), seq(va), t5_table[:, :A_HEADS], tabs["bkt_a"], A_WINDOW, lw["a_sink"]))
    ob = flat(_flash_b(seq(qb), seq(kb), seq(vb)))
    od = flat(_flash_d(seq(qd), seq(kd), seq(vd)))

    oc, lse = [], []
    for g, (q, k, v) in enumerate(((seq(qc0), seq(kc0), seq(vc0)), (qc1, kc1, vc1), (qc2, kc2, vc2))):
        dil = C_GROUPS[g][1]
        table = t5_table[:, A_HEADS + g * C_HEADS:A_HEADS + (g + 1) * C_HEADS]
        streams = lambda a: a.reshape(b * dil, l // dil, 256)
        o_g, lse_g = _banded(streams(q), streams(k), streams(v), table, tabs["bkt_c"][g], C_BAND)
        if dil == 1:
            oc.append(flat(o_g))
            lse.append(flat(lse_g))
        else:
            oc.append(o_g.reshape(b, dil, l // dil, 256))
            lse.append(lse_g.reshape(b, dil, l // dil, 256))

    x_mid, h2, aff = _merge(x, h, oa, ob, oc, lse, od, l, lw)

    cap = EC_CAPACITY_FACTOR * n // N_EXPERTS
    gate, idx = lax.top_k(aff[:, :N_EXPERTS].T, cap)
    ye = _experts(h2[idx], gate[..., None], lw)

    pair_tok = idx.reshape(-1)
    order = jnp.argsort(pair_tok).astype(jnp.int32)
    pad = COMBINE_WIN + LANE
    tok_sorted = jnp.concatenate([pair_tok[order], jnp.full((pad,), -1, jnp.int32)])
    z = ye.reshape(-1, D_MODEL)[jnp.concatenate([order, jnp.zeros((pad,), jnp.int32)])]
    edges = jnp.arange(0, n + 1, TILE_COMBINE, dtype=jnp.int32)
    starts = jnp.sum((pair_tok[None, :] < edges[:, None]).astype(jnp.int32), axis=1)
    return _combine(x_mid, tok_sorted[None, :], z, starts, final_gain)


def _trunk(x, layers, t5_table, norm_final):
    b, l, _ = x.shape
    tabs = _position_tables(l)
    tabs["bkt_a"] = _band_buckets(A_WINDOW, A_WINDOW, 1)
    tabs["bkt_c"] = [_band_buckets(2 * C_BAND, C_BAND, dil) for _, dil in C_GROUPS]
    x = x.reshape(b * l, D_MODEL)
    for i, lw in enumerate(layers):
        final_gain = norm_final.reshape(1, -1).astype(F32) if i == len(layers) - 1 else None
        x = _layer(x, b, l, lw, tabs, t5_table, final_gain)
    return x.reshape(b, l, D_MODEL)


def kernel(x_prompt, x_sample, t5_table, norm_mix, w_in, a_sink, mla_q_norm, mla_w_uq, mla_kv_norm, mla_w_ukv,
           d_q_norm, d_k_norm, w_gate, w_branch, w_out, norm_ffn, w_router, w_exp_gate, w_exp_up, w_exp_down,
           norm_final):
    p = dict(norm_mix=norm_mix, w_in=w_in, a_sink=a_sink, mla_q_norm=mla_q_norm, mla_w_uq=mla_w_uq,
             mla_kv_norm=mla_kv_norm, mla_w_ukv=mla_w_ukv, d_q_norm=d_q_norm, d_k_norm=d_k_norm, w_gate=w_gate,
             w_branch=w_branch, w_out=w_out, norm_ffn=norm_ffn, w_router=w_router, w_exp_gate=w_exp_gate,
             w_exp_up=w_exp_up, w_exp_down=w_exp_down)
    layers = [_pack_layer(l, p) for l in range(w_in.shape[0])]
    t5 = t5_table.astype(F32)
    return (_trunk(x_prompt, layers, t5, norm_final), _trunk(x_sample, layers, t5, norm_final))
```

```python
import functools
import math

import numpy as np
import jax
import jax.numpy as jnp
from jax import lax
from jax.experimental import pallas as pl
from jax.experimental.pallas import tpu as pltpu

F32 = jnp.float32
BF16 = jnp.bfloat16

D_MODEL = 1024
HEAD_DIM = 64
A_HEADS = 4
A_WINDOW = 128
MLA_HEADS = 4
MLA_Q_RANK = 256
MLA_KV_RANK = 128
MLA_NOPE = 64
MLA_ROPE = 32
MLA_V = 64
C_HEADS = 4
C_GROUPS = ((128, 1), (512, 4), (2048, 16))
C_BAND = 64
GRID_W = 64
N_BRANCHES = 4
BRANCH_W = 256
N_BUCKETS = 32
MAX_DISTANCE = 1024
N_EXPERTS = 16
EC_CAPACITY_FACTOR = 2
ROPE_THETA = 10000.0
ROPE_HALF = 16
EPS = 1e-6
NEG = -1e30

LANE = 128
VMEM_LIMIT = 56 * 1024 * 1024

SEG_QA, SEG_KA, SEG_VA = 0, 256, 384
SEG_CQ, SEG_CKV, SEG_KR = 512, 768, 896
SEG_QC, SEG_KC, SEG_VC = 1024, 1792, 2560
SEG_QD, SEG_KD, SEG_VD = 3328, 3584, 3712
IN_COLS_PACKED = 3968

LOG2E = math.log2(math.e)
A_SCALE = HEAD_DIM ** -0.5 * LOG2E
B_SCALE = (MLA_NOPE + MLA_ROPE) ** -0.5 * LOG2E

TILE_PROJ = 512
TILE_MERGE = 256
MERGE_COLS = 256
TILE_EXPERT = 512
TILE_COMBINE = 256
COMBINE_WIN = 768
BAND_TILE = 128
BAND_STEP = 512
FLASH_ROWS = 256
FLASH_KV = 512
FLASH_SUB = 32
FLASH_UNROLL = 8


def _dot(a, b):
    return jnp.dot(a, b, preferred_element_type=F32)


def _dot_nt(a, b):
    return lax.dot_general(a, b, (((1,), (1,)), ((), ())), preferred_element_type=F32)


def _params(*sem):
    return pltpu.CompilerParams(dimension_semantics=sem, vmem_limit_bytes=VMEM_LIMIT)


def _const_spec(shape):
    zeros = (0,) * len(shape)
    return pl.BlockSpec(shape, lambda *_: zeros, pipeline_mode=pl.Buffered(1))


def _smem_spec():
    return pl.BlockSpec(memory_space=pltpu.SMEM)


def _lane_iota(shape):
    return lax.broadcasted_iota(jnp.int32, shape, len(shape) - 1)


def _rope128(x, cos, sin_signed, first_half):
    rot = jnp.where(first_half, pltpu.roll(x, LANE - ROPE_HALF, 1), pltpu.roll(x, ROPE_HALF, 1))
    return x * cos + rot * sin_signed


def _rms(x, gain):
    return x * lax.rsqrt(jnp.mean(x * x, axis=-1, keepdims=True) + EPS) * gain


def _lane_blocks(width):
    return [slice(c * LANE, (c + 1) * LANE) for c in range(width // LANE)]


def _proj_kernel(x_ref, gmix_ref, win_ref, wuq_ref, wukvk_ref, wukvv_ref, gq_ref, gkv_ref, gdq_ref, gdk_ref,
                 g64_ref, oneb_ref, oned_ref, cosb_ref, sinb_ref, cosd_ref, sind_ref,
                 h_ref, qa_ref, ka_ref, va_ref, qb_ref, kb_ref, vb_ref,
                 qc0_ref, kc0_ref, vc0_ref, qc1_ref, kc1_ref, vc1_ref, qc2_ref, kc2_ref, vc2_ref,
                 qd_ref, kd_ref, vd_ref, stage_ref):
    h = _rms(x_ref[...], gmix_ref[...]).astype(BF16)
    h_ref[...] = h

    def seg(lo, width):
        return _dot(h, win_ref[:, lo:lo + width])

    first_half = jnp.bitwise_and(_lane_iota((1, LANE)), 2 * ROPE_HALF - 1) < ROPE_HALF

    qa_ref[...] = (seg(SEG_QA, 256) * A_SCALE).astype(BF16)
    ka_ref[...] = seg(SEG_KA, 128).astype(BF16)
    va_ref[...] = seg(SEG_VA, 128).astype(BF16)

    cosb, sinb = cosb_ref[...], sinb_ref[...]
    cqn = _rms(seg(SEG_CQ, MLA_Q_RANK), gq_ref[...]).astype(BF16)
    qm = _dot(cqn, wuq_ref[...])
    for blk in _lane_blocks(MLA_HEADS * LANE):
        qb_ref[:, blk] = (_rope128(qm[:, blk], cosb, sinb, first_half) * B_SCALE).astype(BF16)
    ckvn = _rms(seg(SEG_CKV, MLA_KV_RANK), gkv_ref[...]).astype(BF16)
    k_pe = _rope128(seg(SEG_KR, LANE), cosb, sinb, first_half)
    kn = _dot(ckvn, wukvk_ref[...])
    for blk in _lane_blocks(MLA_HEADS * LANE):
        kb_ref[:, blk] = (kn[:, blk] + k_pe).astype(BF16)
    vb_ref[...] = (_dot(ckvn, wukvv_ref[...]) + oneb_ref[...]).astype(BF16)

    t = x_ref.shape[0]
    c_out = ((qc0_ref, kc0_ref, vc0_ref), (qc1_ref, kc1_ref, vc1_ref), (qc2_ref, kc2_ref, vc2_ref))
    for a, (lo, scale) in enumerate(((SEG_QC, A_SCALE), (SEG_KC, None), (SEG_VC, None))):
        z = seg(lo, 768)
        if scale is not None:
            z = z * scale
        c_out[0][a][...] = z[:, :256].astype(BF16)
        for g in (1, 2):
            dil = C_GROUPS[g][1]
            for c, blk in enumerate(_lane_blocks(256)):
                slot = (a * 2 + (g - 1)) * 2 + c
                stage_ref[slot] = z[:, g * 256 + c * LANE:g * 256 + (c + 1) * LANE]
                for r in range(dil):
                    c_out[g][a][r, :, blk] = stage_ref[slot, pl.ds(r, t // dil, stride=dil), :].astype(BF16)

    cosd, sind = cosd_ref[...], sind_ref[...]
    g64 = g64_ref[...]

    def head_norm(z, gain):
        sq = z * z
        hi = sq.astype(BF16)
        lo = (sq - hi.astype(F32)).astype(BF16)
        ms = _dot(jnp.concatenate([hi, lo], axis=1), g64)
        return z * lax.rsqrt(ms + EPS) * gain

    zq = seg(SEG_QD, 256)
    for blk in _lane_blocks(256):
        y = head_norm(zq[:, blk], gdq_ref[...])
        qd_ref[:, blk] = (_rope128(y, cosd, sind, first_half) * A_SCALE).astype(BF16)
    kd_ref[...] = _rope128(head_norm(seg(SEG_KD, 128), gdk_ref[...]), cosd, sind, first_half).astype(BF16)
    vd_ref[...] = (seg(SEG_VD, 256) + oned_ref[...]).astype(BF16)


def _proj(x, batch, seq_len, lw, tabs):
    n = x.shape[0]
    t = TILE_PROJ
    per_seq = seq_len // t
    row = lambda w: pl.BlockSpec((t, w), lambda i: (i, 0))
    tab = pl.BlockSpec((t, LANE), lambda i: (i % per_seq, 0))
    flat = lambda w: (row(w), jax.ShapeDtypeStruct((n, w), BF16))

    def planes(dil):
        return (pl.BlockSpec((None, dil, t // dil, 256), lambda i: (i // per_seq, 0, i % per_seq, 0)),
                jax.ShapeDtypeStruct((batch, dil, seq_len // dil, 256), BF16))

    outs = ([flat(D_MODEL), flat(256), flat(128), flat(128), flat(512), flat(512), flat(512)]
            + [flat(256)] * 3 + [planes(C_GROUPS[1][1])] * 3 + [planes(C_GROUPS[2][1])] * 3
            + [flat(256), flat(128), flat(256)])
    return pl.pallas_call(
        _proj_kernel,
        grid=(n // t,),
        in_specs=[row(D_MODEL), _const_spec((1, D_MODEL)), _const_spec((D_MODEL, IN_COLS_PACKED)),
                  _const_spec((MLA_Q_RANK, 512)), _const_spec((MLA_KV_RANK, 512)), _const_spec((MLA_KV_RANK, 512)),
                  _const_spec((1, MLA_Q_RANK)), _const_spec((1, MLA_KV_RANK)), _const_spec((1, LANE)),
                  _const_spec((1, LANE)), _const_spec((2 * LANE, LANE)), _const_spec((1, 512)), _const_spec((1, 256)),
                  tab, tab, tab, tab],
        out_specs=[o[0] for o in outs],
        out_shape=[o[1] for o in outs],
        scratch_shapes=[pltpu.VMEM((12, t, LANE), F32)],
        compiler_params=_params("arbitrary"),
        name="norm_in_proj",
    )(x, lw["g_mix"], lw["w_in"], lw["w_uq"], lw["w_ukv_k"], lw["w_ukv_v"], lw["g_q"], lw["g_kv"], lw["g_dq"],
      lw["g_dk"], tabs["g64"], tabs["ones_b"], tabs["ones_d"], tabs["cos_b"], tabs["sin_b"], tabs["cos_d"],
      tabs["sin_d"])


def _build_bias(tab_ref, bkt_ref, bias_ref, n_heads, band):
    bkt = bkt_ref[...]
    row = lax.broadcasted_iota(jnp.int32, bkt.shape, 0)
    col = lax.broadcasted_iota(jnp.int32, bkt.shape, 1)
    valid = jnp.abs(col - band - row) <= band
    for hh in range(n_heads):
        def body(bb, acc, hh=hh):
            return jnp.where(bkt == bb, tab_ref[bb, hh], acc)
        acc = lax.fori_loop(0, N_BUCKETS, body, jnp.zeros(bkt.shape, F32))
        bias_ref[hh] = jnp.where(valid, acc * LOG2E, NEG)


def _banded_kernel(*refs, band, n_heads, paired_heads, has_sink):
    refs = list(refs)
    tab_ref = refs.pop(0)
    sink_ref = refs.pop(0) if has_sink else None
    bkt_ref, q_ref, kl_ref, kc_ref, kr_ref, vl_ref, vc_ref, vr_ref, o_ref = refs[:9]
    lse_ref = None if has_sink else refs[9]
    bias_ref, s_ref = refs[-2:]
    n = pl.program_id(1)
    last = pl.num_programs(1) - 1

    @pl.when((pl.program_id(0) == 0) & (n == 0))
    def _():
        _build_bias(tab_ref, bkt_ref, bias_ref, n_heads, band)

    t = BAND_TILE
    win = t + 2 * band
    n_sub = q_ref.shape[0] // t
    col = _lane_iota((t, win))
    is_lo = _lane_iota((1, LANE)) < HEAD_DIM
    zero = jnp.zeros((t, LANE), BF16)
    for blk in range(2):
        lanes = slice(blk * LANE, (blk + 1) * LANE)
        kv_lanes = slice(0, LANE) if paired_heads else lanes
        kall = jnp.concatenate([kl_ref[:, kv_lanes], kc_ref[:, kv_lanes], kr_ref[:, kv_lanes]], axis=0)
        for j in range(n_sub):
            qb = q_ref[j * t:(j + 1) * t, lanes]
            for half in range(2):
                qm = jnp.where(is_lo if half == 0 else jnp.logical_not(is_lo), qb, zero)
                s_ref[(blk * n_sub + j) * 2 + half] = _dot_nt(qm, kall[j * t:j * t + win])
    for blk in range(2):
        lanes = slice(blk * LANE, (blk + 1) * LANE)
        kv_lanes = slice(0, LANE) if paired_heads else lanes
        vall = jnp.concatenate([vl_ref[:, kv_lanes], vc_ref[:, kv_lanes], vr_ref[:, kv_lanes]], axis=0)
        for j in range(n_sub):
            rows = slice(j * t, (j + 1) * t)
            vwin = vall[j * t:j * t + win]
            outs, lses = [], []
            for half in range(2):
                head = blk + 2 * half if paired_heads else 2 * blk + half
                s = s_ref[(blk * n_sub + j) * 2 + half] + bias_ref[head]
                if j == 0:
                    s = jnp.where(col >= jnp.where(n > 0, 0, band), s, NEG)
                if j == n_sub - 1:
                    s = jnp.where(col < jnp.where(n < last, win, t + band), s, NEG)
                m = jnp.max(s, axis=-1, keepdims=True)
                if has_sink:
                    sk = sink_ref[head] * LOG2E
                    m = jnp.maximum(m, sk)
                p = jnp.exp2(s - m)
                denom = jnp.sum(p, axis=-1, keepdims=True)
                if has_sink:
                    denom = denom + jnp.exp2(sk - m)
                outs.append(_dot(p.astype(BF16), vwin) / denom)
                if lse_ref is not None:
                    lses.append(m * (1.0 / LOG2E) + jnp.log(denom))
            o_ref[rows, lanes] = jnp.where(is_lo, outs[0], outs[1]).astype(BF16)
            if lse_ref is not None:
                lse_ref[rows, lanes] = jnp.where(is_lo, lses[0], lses[1])


def _banded(q, k, v, table, bkt, band, sink=None):
    s, m, _ = q.shape
    kv_w = k.shape[-1]
    rows = min(BAND_STEP, m)
    per = rows // band
    nh = m // band
    halo = lambda f: pl.BlockSpec((None, band, kv_w), lambda si, ni: (si, f(ni), 0))
    left, right = (lambda ni: jnp.maximum(per * ni - 1, 0)), (lambda ni: jnp.minimum(per * (ni + 1), nh - 1))
    mid = lambda w: pl.BlockSpec((None, rows, w), lambda si, ni: (si, ni, 0))
    win = BAND_TILE + 2 * band
    has_sink = sink is not None
    out_bf = jax.ShapeDtypeStruct((s, m, 256), BF16)
    return pl.pallas_call(
        functools.partial(_banded_kernel, band=band, n_heads=table.shape[1], paired_heads=has_sink,
                          has_sink=has_sink),
        grid=(s, m // rows),
        in_specs=[_smem_spec()] + ([_smem_spec()] if has_sink else [])
                 + [_const_spec((BAND_TILE, win)), mid(256), halo(left), mid(kv_w), halo(right), halo(left),
                    mid(kv_w), halo(right)],
        out_specs=mid(256) if has_sink else [mid(256), mid(256)],
        out_shape=out_bf if has_sink else [out_bf, jax.ShapeDtypeStruct((s, m, 256), F32)],
        scratch_shapes=[pltpu.VMEM((table.shape[1], BAND_TILE, win), F32),
                        pltpu.VMEM((4 * (rows // BAND_TILE), BAND_TILE, win), F32)],
        compiler_params=_params("arbitrary", "arbitrary"),
        name="attn_window_sink" if has_sink else "attn_dilated",
    )(*([table] + ([sink] if has_sink else []) + [bkt, q, k, k, k, v, v, v]))


def _flash_chains(qs, k_refs, v_refs, s_ref, p_ref, m_ref, a_ref, acc_ref):
    nc = len(qs)
    rows = qs[0].shape[0]
    n_chunks = k_refs[0].shape[0] // FLASH_KV
    unroll = min(FLASH_UNROLL, n_chunks)
    ncol = FLASH_KV // LANE

    def scores(slot, chunk):
        start = pl.multiple_of(chunk * FLASH_KV, FLASH_KV)
        for c in range(nc):
            s_ref[slot, c] = _dot_nt(qs[c], k_refs[c][pl.ds(start, FLASH_KV), :])

    def consume(slot, chunk):
        start = pl.multiple_of(chunk * FLASH_KV, FLASH_KV)
        for c in range(nc):
            for r0 in range(0, rows, FLASH_SUB):
                rs = slice(r0, r0 + FLASH_SUB)
                sb = [s_ref[slot, c, rs, j * LANE:(j + 1) * LANE] for j in range(ncol)]
                smax = functools.reduce(jnp.maximum, sb)
                m_old = m_ref[c, rs, :]
                m_new = jnp.maximum(m_old, jnp.max(smax, axis=-1, keepdims=True))
                m_ref[c, rs, :] = m_new
                a_ref[c, rs, :] = jnp.exp2(m_old - m_new)
                for j in range(ncol):
                    p_ref[c, rs, j * LANE:(j + 1) * LANE] = jnp.exp2(sb[j] - m_new).astype(BF16)
            acc_ref[c] = a_ref[c] * acc_ref[c] + _dot(p_ref[c], v_refs[c][pl.ds(start, FLASH_KV), :])

    m_ref[...] = jnp.full(m_ref.shape, NEG, F32)
    acc_ref[...] = jnp.zeros(acc_ref.shape, F32)
    scores(0, 0)

    def trip(i, carry):
        base = i * unroll
        for u in range(unroll):
            scores((u + 1) % 2, base + u + 1)
            consume(u % 2, base + u)
        return carry

    lax.fori_loop(0, n_chunks // unroll - 1, trip, 0)
    base = n_chunks - unroll
    for u in range(unroll):
        if u + 1 < unroll:
            scores((u + 1) % 2, base + u + 1)
        consume(u % 2, base + u)
    return [acc_ref[c] / pltpu.roll(acc_ref[c], HEAD_DIM, 1) for c in range(nc)]


def _flash_scratch(nc, rows):
    stat = pltpu.VMEM((nc, rows, LANE), F32)
    return [pltpu.VMEM((2, nc, rows, FLASH_KV), F32), pltpu.VMEM((nc, rows, FLASH_KV), BF16), stat, stat, stat]


def _flash_b_kernel(qlo_ref, qhi_ref, klo_ref, khi_ref, vlo_ref, vhi_ref, o_ref, *scratch):
    o_lo, o_hi = _flash_chains([qlo_ref[...], qhi_ref[...]], [klo_ref, khi_ref], [vlo_ref, vhi_ref], *scratch)
    is_lo = _lane_iota((1, LANE)) < HEAD_DIM
    o_ref[...] = jnp.where(is_lo, o_lo, o_hi).astype(BF16)


def _flash_b(q, k, v):
    b, l, _ = q.shape
    tq = FLASH_ROWS
    qs = lambda off: pl.BlockSpec((None, tq, LANE), lambda bi, pi, qi: (bi, qi, 2 * pi + off))
    ks = lambda off: pl.BlockSpec((None, l, LANE), lambda bi, pi, qi: (bi, 0, 2 * pi + off))
    return pl.pallas_call(
        _flash_b_kernel,
        grid=(b, 2, l // tq),
        in_specs=[qs(0), qs(1), ks(0), ks(1), ks(0), ks(1)],
        out_specs=pl.BlockSpec((None, tq, LANE), lambda bi, pi, qi: (bi, qi, pi)),
        out_shape=jax.ShapeDtypeStruct((b, l, 256), BF16),
        scratch_shapes=_flash_scratch(2, FLASH_ROWS),
        compiler_params=_params("arbitrary", "arbitrary", "arbitrary"),
        name="attn_dense_mla",
    )(q, q, k, k, v, v)


def _flash_d_kernel(q_ref, k_ref, vlo_ref, vhi_ref, o_ref, *scratch):
    q = q_ref[...]
    tq = q.shape[0]
    is_lo = _lane_iota((1, LANE)) < HEAD_DIM
    zero = jnp.zeros((tq, LANE), BF16)
    qb = [q[:, :LANE], q[:, LANE:]]
    qs = [jnp.concatenate([jnp.where(keep, qb[0], zero), jnp.where(keep, qb[1], zero)], axis=0)
          for keep in (is_lo, jnp.logical_not(is_lo))]
    o_lo, o_hi = _flash_chains(qs, [k_ref, k_ref], [vlo_ref, vhi_ref], *scratch)
    o_ref[:, :LANE] = jnp.where(is_lo, o_lo[:tq], o_hi[:tq]).astype(BF16)
    o_ref[:, LANE:] = jnp.where(is_lo, o_lo[tq:], o_hi[tq:]).astype(BF16)


def _flash_d(q, k, v):
    b, l, _ = q.shape
    tq = FLASH_ROWS // 2
    whole = lambda blk: pl.BlockSpec((None, l, LANE), lambda bi, qi: (bi, 0, blk))
    tile = pl.BlockSpec((None, tq, 256), lambda bi, qi: (bi, qi, 0))
    return pl.pallas_call(
        _flash_d_kernel,
        grid=(b, l // tq),
        in_specs=[tile, whole(0), whole(0), whole(1)],
        out_specs=tile,
        out_shape=jax.ShapeDtypeStruct((b, l, 256), BF16),
        scratch_shapes=_flash_scratch(2, FLASH_ROWS),
        compiler_params=_params("arbitrary", "arbitrary"),
        name="attn_dense_axial",
    )(q, k, v, v)


def _merge_kernel(x_ref, h_ref, oa_ref, ob_ref, oc0_ref, l0_ref, oc1_ref, l1_ref, oc2_ref, l2_ref, od_ref,
                  wg_ref, wb_ref, wo_ref, gffn_ref, wr_ref, xo_ref, h2_ref, aff_ref, stage_ref, merged_ref):
    h = h_ref[...]
    t = h.shape[0]

    def token_order(ref, dil, slot):
        cols = []
        for c, blk in enumerate(_lane_blocks(256)):
            for r in range(dil):
                stage_ref[slot + c, pl.ds(r, t // dil, stride=dil), :] = ref[r, :, blk].astype(F32)
            cols.append(stage_ref[slot + c])
        return jnp.concatenate(cols, axis=1)

    o = [oc0_ref[...].astype(F32), token_order(oc1_ref, C_GROUPS[1][1], 0), token_order(oc2_ref, C_GROUPS[2][1], 2)]
    lse = [l0_ref[...], token_order(l1_ref, C_GROUPS[1][1], 4), token_order(l2_ref, C_GROUPS[2][1], 6)]
    mx = jnp.maximum(jnp.maximum(lse[0], lse[1]), lse[2])
    e = [jnp.exp(l - mx) for l in lse]
    den = e[0] + e[1] + e[2]
    oc = ((e[0] / den) * o[0] + (e[1] / den) * o[1] + (e[2] / den) * o[2]).astype(BF16)
    branches = (oa_ref[...], ob_ref[...], oc, od_ref[...])
    for cols in (slice(c, c + MERGE_COLS) for c in range(0, D_MODEL, MERGE_COLS)):
        merged = None
        for bi in (0, 1, 3, 2):
            term = jax.nn.sigmoid(_dot(h, wg_ref[bi, :, cols])) * _dot(branches[bi], wb_ref[bi, :, cols])
            merged = term if merged is None else merged + term
        merged_ref[:, cols] = merged.astype(BF16)
    xn = x_ref[...] + _dot(merged_ref[...], wo_ref[...])
    xo_ref[...] = xn
    hn = _rms(xn, gffn_ref[...]).astype(BF16)
    h2_ref[...] = hn
    logits = jnp.where(_lane_iota((1, LANE)) < N_EXPERTS, _dot(hn, wr_ref[...]), NEG)
    ex = jnp.exp(logits - jnp.max(logits, axis=-1, keepdims=True))
    aff_ref[...] = ex / jnp.sum(ex, axis=-1, keepdims=True)


def _merge(x, h, oa, ob, oc, lse, od, seq_len, lw):
    n = x.shape[0]
    t = TILE_MERGE
    per_seq = seq_len // t
    row = lambda w: pl.BlockSpec((t, w), lambda i: (i, 0))
    planes = lambda dil: pl.BlockSpec((None, dil, t // dil, 256), lambda i: (i // per_seq, 0, i % per_seq, 0))
    p1, p2 = planes(C_GROUPS[1][1]), planes(C_GROUPS[2][1])
    return pl.pallas_call(
        _merge_kernel,
        grid=(n // t,),
        in_specs=[row(D_MODEL), row(D_MODEL), row(BRANCH_W), row(BRANCH_W), row(BRANCH_W), row(BRANCH_W),
                  p1, p1, p2, p2, row(BRANCH_W),
                  _const_spec((N_BRANCHES, D_MODEL, D_MODEL)), _const_spec((N_BRANCHES, BRANCH_W, D_MODEL)),
                  _const_spec((D_MODEL, D_MODEL)), _const_spec((1, D_MODEL)), _const_spec((D_MODEL, LANE))],
        out_specs=[row(D_MODEL), row(D_MODEL), row(LANE)],
        out_shape=[jax.ShapeDtypeStruct((n, D_MODEL), F32), jax.ShapeDtypeStruct((n, D_MODEL), BF16),
                   jax.ShapeDtypeStruct((n, LANE), F32)],
        scratch_shapes=[pltpu.VMEM((8, t, LANE), F32), pltpu.VMEM((t, D_MODEL), BF16)],
        compiler_params=_params("arbitrary"),
        name="gated_merge_router",
    )(x, h, oa, ob, oc[0], lse[0], oc[1], lse[1], oc[2], lse[2], od,
      lw["w_gate"], lw["w_branch"], lw["w_out"], lw["g_ffn"], lw["w_router"])


def _expert_kernel(xe_ref, gate_ref, wg_ref, wu_ref, wd_ref, ye_ref, wg_bf, wu_bf, wd_bf):
    @pl.when(pl.program_id(1) == 0)
    def _():
        wg_bf[...] = wg_ref[...].astype(BF16)
        wu_bf[...] = wu_ref[...].astype(BF16)
        wd_bf[...] = wd_ref[...].astype(BF16)

    xe = xe_ref[...]
    he = (jax.nn.silu(_dot(xe, wg_bf[...])) * _dot(xe, wu_bf[...])).astype(BF16)
    ye_ref[...] = _dot(he, wd_bf[...]) * gate_ref[...]


def _experts(xe, gate, lw):
    e, cap, d = xe.shape
    t = min(TILE_EXPERT, cap)
    ff = lw["w_exp_gate"].shape[-1]
    wspec = lambda a, b: pl.BlockSpec((None, a, b), lambda ei, ti: (ei, 0, 0))
    return pl.pallas_call(
        _expert_kernel,
        grid=(e, cap // t),
        in_specs=[pl.BlockSpec((None, t, d), lambda ei, ti: (ei, ti, 0)),
                  pl.BlockSpec((None, t, 1), lambda ei, ti: (ei, ti, 0)),
                  wspec(d, ff), wspec(d, ff), wspec(ff, d)],
        out_specs=pl.BlockSpec((None, t, d), lambda ei, ti: (ei, ti, 0)),
        out_shape=jax.ShapeDtypeStruct((e, cap, d), F32),
        scratch_shapes=[pltpu.VMEM((d, ff), BF16), pltpu.VMEM((d, ff), BF16), pltpu.VMEM((ff, d), BF16)],
        compiler_params=_params("arbitrary", "arbitrary"),
        name="expert_swiglu",
    )(xe, gate, lw["w_exp_gate"], lw["w_exp_up"], lw["w_exp_down"])


def _combine_kernel(starts_ref, x_ref, tok_hbm, z_hbm, *refs, final_norm):
    gain_ref = refs[0] if final_norm else None
    o_ref, tok_buf, z_buf, acc_ref, slot_ref, sem = refs[-6:]
    b = pl.program_id(0)
    last = pl.num_programs(0) - 1
    t = x_ref.shape[0]
    w = COMBINE_WIN

    def span(blk):
        base = lax.shift_left(lax.shift_right_logical(starts_ref[blk], 7), 7)
        return base, lax.div(starts_ref[blk + 1] - base + (w - 1), w)

    def copies(row0, slot):
        row0 = pl.multiple_of(row0, LANE)
        return (pltpu.make_async_copy(tok_hbm.at[:, pl.ds(row0, w)], tok_buf.at[slot], sem.at[0, slot]),
                pltpu.make_async_copy(z_hbm.at[pl.ds(row0, w), :], z_buf.at[slot], sem.at[1, slot]))

    def start(row0, slot):
        for c in copies(row0, slot):
            c.start()

    def start_next_block(slot):
        @pl.when(b < last)
        def _():
            next_base, next_win = span(b + 1)

            @pl.when(next_win > 0)
            def _():
                start(next_base, slot)

    base, n_win = span(b)

    @pl.when(b == 0)
    def _():
        slot_ref[0] = 0

        @pl.when(n_win > 0)
        def _():
            start(base, 0)

    first = slot_ref[0]
    acc_ref[...] = jnp.zeros(acc_ref.shape, F32)
    ids = b * t + lax.broadcasted_iota(jnp.int32, (t, 1), 0)

    def window(i, carry):
        slot = lax.rem(first + i, 2)
        for c in copies(base + i * w, slot):
            c.wait()

        @pl.when(i + 1 < n_win)
        def _():
            start(base + (i + 1) * w, 1 - slot)

        @pl.when(i + 1 == n_win)
        def _():
            start_next_block(1 - slot)
            slot_ref[0] = 1 - slot

        onehot = jnp.where(tok_buf[slot] == ids, 1.0, 0.0).astype(BF16)
        z = z_buf[slot]
        hi = z.astype(BF16)
        lo = (z - hi.astype(F32)).astype(BF16)
        acc_ref[...] += _dot(onehot, hi) + _dot(onehot, lo)
        return carry

    lax.fori_loop(0, n_win, window, 0)

    @pl.when(n_win == 0)
    def _():
        start_next_block(first)

    out = x_ref[...] + acc_ref[...]
    o_ref[...] = _rms(out, gain_ref[...]) if final_norm else out


def _combine(x, tok_sorted, z, starts, final_gain=None):
    n, d = x.shape
    t = TILE_COMBINE
    w = COMBINE_WIN
    final_norm = final_gain is not None
    return pl.pallas_call(
        functools.partial(_combine_kernel, final_norm=final_norm),
        grid_spec=pltpu.PrefetchScalarGridSpec(
            num_scalar_prefetch=1,
            grid=(n // t,),
            in_specs=[pl.BlockSpec((t, d), lambda i, s: (i, 0)), pl.BlockSpec(memory_space=pl.ANY),
                      pl.BlockSpec(memory_space=pl.ANY)]
                     + ([pl.BlockSpec((1, d), lambda i, s: (0, 0))] if final_norm else []),
            out_specs=pl.BlockSpec((t, d), lambda i, s: (i, 0)),
            scratch_shapes=[pltpu.VMEM((2, 1, w), jnp.int32), pltpu.VMEM((2, w, d), F32), pltpu.VMEM((t, d), F32),
                            pltpu.SMEM((1,), jnp.int32), pltpu.SemaphoreType.DMA((2, 2))],
        ),
        out_shape=jax.ShapeDtypeStruct((n, d), F32),
        compiler_params=_params("arbitrary"),
        name="expert_combine",
    )(*([starts, x, tok_sorted, z] + ([final_gain] if final_norm else [])))


def _t5_bucket_np(rel):
    nb = N_BUCKETS // 2
    max_exact = nb // 2
    n = np.abs(rel)
    large = max_exact + (np.log(np.maximum(n, 1).astype(np.float64) / max_exact)
                         / math.log(MAX_DISTANCE / max_exact) * (nb - max_exact)).astype(np.int32)
    large = np.minimum(large, nb - 1)
    return (np.where(rel > 0, nb, 0) + np.where(n < max_exact, n, large)).astype(np.int32)


def _band_buckets(rows, band, dil):
    rel = np.arange(rows + 2 * band, dtype=np.int32)[None, :] - band - np.arange(rows, dtype=np.int32)[:, None]
    return jnp.asarray(_t5_bucket_np(rel * dil))


def _position_tables(seq_len):
    t = jnp.arange(seq_len, dtype=jnp.int32)
    inv = ROPE_THETA ** (-jnp.arange(ROPE_HALF, dtype=F32) / ROPE_HALF)

    def cs(pos):
        ang = pos.astype(F32)[:, None] * inv[None, :]
        return jnp.cos(ang), jnp.sin(ang)

    def group(c, s):
        return jnp.concatenate([c, c], axis=1), jnp.concatenate([-s, s], axis=1)

    ones, zeros = jnp.ones((seq_len, 32), F32), jnp.zeros((seq_len, 32), F32)
    cb, sb = group(*cs(t))
    cr, sr = group(*cs(t // GRID_W))
    cc, sc = group(*cs(t % GRID_W))
    g64 = np.kron(np.eye(2, dtype=np.float32), np.full((HEAD_DIM, HEAD_DIM), 1.0 / HEAD_DIM, np.float32))
    pad_ones = np.tile(np.concatenate([np.zeros(64), np.ones(128), np.zeros(64)]).astype(np.float32), 2)
    return {
        "cos_b": jnp.concatenate([ones, ones, cb, ones], axis=1),
        "sin_b": jnp.concatenate([zeros, zeros, sb, zeros], axis=1),
        "cos_d": jnp.concatenate([cr, cc, cr, cc], axis=1),
        "sin_d": jnp.concatenate([sr, sc, sr, sc], axis=1),
        "g64": jnp.asarray(np.concatenate([g64, g64], axis=0), BF16),
        "ones_b": jnp.asarray(pad_ones.reshape(1, 512)),
        "ones_d": jnp.asarray(pad_ones[:256].reshape(1, 256)),
    }


def _pad_value_heads(w):
    d, heads, _ = w.shape
    z = jnp.zeros((d, heads // 2, HEAD_DIM), w.dtype)
    return jnp.stack([w[:, 0::2], z, z, w[:, 1::2]], axis=2).reshape(d, heads * LANE)


def _pack_layer(l, p):
    w_in = p["w_in"][l]
    split = np.cumsum([0, 256, 128, 128, 256, 128, 32, 768, 768, 768, 256, 128, 128])
    qa, ka, va, cq, ckv, kr, qc, kc, vc, qd, kd, vd = [w_in[:, split[i]:split[i + 1]] for i in range(12)]

    def pair_heads(w):
        d = w.shape[0]
        return w.reshape(d, 4, HEAD_DIM)[:, jnp.array([0, 2, 1, 3])].reshape(d, 256)

    z = lambda w: jnp.zeros((w_in.shape[0], w), F32)
    packed = jnp.concatenate([pair_heads(qa), ka, va, cq, ckv, z(64), kr, z(32), qc, kc, vc, pair_heads(qd), kd,
                              _pad_value_heads(vd.reshape(-1, 2, HEAD_DIM))], axis=1)
    uq = p["mla_w_uq"][l].reshape(MLA_Q_RANK, MLA_HEADS, MLA_NOPE + MLA_ROPE)
    uq = jnp.pad(uq, ((0, 0), (0, 0), (0, LANE - MLA_NOPE - MLA_ROPE))).reshape(MLA_Q_RANK, MLA_HEADS * LANE)
    ukv = p["mla_w_ukv"][l].reshape(MLA_KV_RANK, MLA_HEADS, MLA_NOPE + MLA_V)
    ukv_k = jnp.pad(ukv[:, :, :MLA_NOPE], ((0, 0), (0, 0), (0, LANE - MLA_NOPE))).reshape(MLA_KV_RANK, MLA_HEADS * LANE)
    ukv_v = _pad_value_heads(ukv[:, :, MLA_NOPE:])
    wb = p["w_branch"][l]
    pair_rows = lambda w: w.reshape(4, HEAD_DIM, D_MODEL)[jnp.array([0, 2, 1, 3])].reshape(BRANCH_W, D_MODEL)
    wb = jnp.stack([pair_rows(wb[0]), wb[1], wb[2], pair_rows(wb[3])])
    wr = jnp.pad(p["w_router"][l], ((0, 0), (0, LANE - N_EXPERTS)))
    row = lambda g: g.reshape(1, -1).astype(F32)
    return {
        "g_mix": row(p["norm_mix"][l]), "w_in": packed.astype(BF16), "w_uq": uq.astype(BF16),
        "w_ukv_k": ukv_k.astype(BF16), "w_ukv_v": ukv_v.astype(BF16),
        "g_q": row(p["mla_q_norm"][l]), "g_kv": row(p["mla_kv_norm"][l]),
        "g_dq": row(jnp.tile(p["d_q_norm"][l], 2)), "g_dk": row(jnp.tile(p["d_k_norm"][l], 2)),
        "a_sink": p["a_sink"][l].astype(F32),
        "w_gate": p["w_gate"][l].astype(BF16), "w_branch": wb.astype(BF16), "w_out": p["w_out"][l].astype(BF16),
        "g_ffn": row(p["norm_ffn"][l]), "w_router": wr.astype(BF16),
        "w_exp_gate": p["w_exp_gate"][l], "w_exp_up": p["w_exp_up"][l], "w_exp_down": p["w_exp_down"][l],
    }


def _layer(x, b, l, lw, tabs, t5_table, final_gain=None):
    n = b * l
    (h, qa, ka, va, qb, kb, vb, qc0, kc0, vc0, qc1, kc1, vc1, qc2, kc2, vc2, qd, kd, vd) = _proj(x, b, l, lw, tabs)
    seq = lambda a: a.reshape(b, l, a.shape[-1])
    flat = lambda a: a.reshape(n, a.shape[-1])

    oa = flat(_banded(seq(qa), seq(ka), seq(va), t5_table[:, :A_HEADS], tabs["bkt_a"], A_WINDOW, lw["a_sink"]))
    ob = flat(_flash_b(seq(qb), seq(kb), seq(vb)))
    od = flat(_flash_d(seq(qd), seq(kd), seq(vd)))

    oc, lse = [], []
    for g, (q, k, v) in enumerate(((seq(qc0), seq(kc0), seq(vc0)), (qc1, kc1, vc1), (qc2, kc2, vc2))):
        dil = C_GROUPS[g][1]
        table = t5_table[:, A_HEADS + g * C_HEADS:A_HEADS + (g + 1) * C_HEADS]
        streams = lambda a: a.reshape(b * dil, l // dil, 256)
        o_g, lse_g = _banded(streams(q), streams(k), streams(v), table, tabs["bkt_c"][g], C_BAND)
        if dil == 1:
            oc.append(flat(o_g))
            lse.append(flat(lse_g))
        else:
            oc.append(o_g.reshape(b, dil, l // dil, 256))
            lse.append(lse_g.reshape(b, dil, l // dil, 256))

    x_mid, h2, aff = _merge(x, h, oa, ob, oc, lse, od, l, lw)

    cap = EC_CAPACITY_FACTOR * n // N_EXPERTS
    gate, idx = lax.top_k(aff[:, :N_EXPERTS].T, cap)
    ye = _experts(h2[idx], gate[..., None], lw)

    pair_tok = idx.reshape(-1)
    order = jnp.argsort(pair_tok).astype(jnp.int32)
    pad = COMBINE_WIN + LANE
    tok_sorted = jnp.concatenate([pair_tok[order], jnp.full((pad,), -1, jnp.int32)])
    z = ye.reshape(-1, D_MODEL)[jnp.concatenate([order, jnp.zeros((pad,), jnp.int32)])]
    edges = jnp.arange(0, n + 1, TILE_COMBINE, dtype=jnp.int32)
    starts = jnp.sum((pair_tok[None, :] < edges[:, None]).astype(jnp.int32), axis=1)
    return _combine(x_mid, tok_sorted[None, :], z, starts, final_gain)


def _trunks(xs, layers, t5_table, norm_final):
    shapes = [x.shape[:2] for x in xs]
    tabs = []
    for _, l in shapes:
        tab = _position_tables(l)
        tab["bkt_a"] = _band_buckets(A_WINDOW, A_WINDOW, 1)
        tab["bkt_c"] = [_band_buckets(2 * C_BAND, C_BAND, dil) for _, dil in C_GROUPS]
        tabs.append(tab)
    xs = [x.reshape(-1, D_MODEL) for x in xs]
    for i, lw in enumerate(layers):
        final_gain = norm_final.reshape(1, -1).astype(F32) if i == len(layers) - 1 else None
        xs = [_layer(x, b, l, lw, tab, t5_table, final_gain) for x, (b, l), tab in zip(xs, shapes, tabs)]
    return tuple(x.reshape(b, l, D_MODEL) for x, (b, l) in zip(xs, shapes))


def kernel(x_prompt, x_sample, t5_table, norm_mix, w_in, a_sink, mla_q_norm, mla_w_uq, mla_kv_norm, mla_w_ukv,
           d_q_norm, d_k_norm, w_gate, w_branch, w_out, norm_ffn, w_router, w_exp_gate, w_exp_up, w_exp_down,
           norm_final):
    p = dict(norm_mix=norm_mix, w_in=w_in, a_sink=a_sink, mla_q_norm=mla_q_norm, mla_w_uq=mla_w_uq,
             mla_kv_norm=mla_kv_norm, mla_w_ukv=mla_w_ukv, d_q_norm=d_q_norm, d_k_norm=d_k_norm, w_gate=w_gate,
             w_branch=w_branch, w_out=w_out, norm_ffn=norm_ffn, w_router=w_router, w_exp_gate=w_exp_gate,
             w_exp_up=w_exp_up, w_exp_down=w_exp_down)
    layers = [_pack_layer(l, p) for l in range(w_in.shape[0])]
    t5 = t5_table.astype(F32)
    return _trunks((x_prompt, x_sample), layers, t5, norm_final)
```

```python
import functools
import math

import numpy as np
import jax
import jax.numpy as jnp
from jax import lax
from jax.experimental import pallas as pl
from jax.experimental.pallas import tpu as pltpu

F32 = jnp.float32
BF16 = jnp.bfloat16

D_MODEL = 1024
HEAD_DIM = 64
A_HEADS = 4
A_WINDOW = 128
MLA_HEADS = 4
MLA_Q_RANK = 256
MLA_KV_RANK = 128
MLA_NOPE = 64
MLA_ROPE = 32
MLA_V = 64
C_HEADS = 4
C_GROUPS = ((128, 1), (512, 4), (2048, 16))
C_BAND = 64
GRID_W = 64
N_BRANCHES = 4
BRANCH_W = 256
N_BUCKETS = 32
MAX_DISTANCE = 1024
N_EXPERTS = 16
EC_CAPACITY_FACTOR = 2
ROPE_THETA = 10000.0
ROPE_HALF = 16
EPS = 1e-6
NEG = -1e30

LANE = 128
VMEM_LIMIT = 56 * 1024 * 1024

SEG_QA, SEG_KA, SEG_VA = 0, 256, 384
SEG_CQ, SEG_CKV, SEG_KR = 512, 768, 896
SEG_QC, SEG_KC, SEG_VC = 1024, 1792, 2560
SEG_QD, SEG_KD, SEG_VD = 3328, 3584, 3712
IN_COLS_PACKED = 3968

LOG2E = math.log2(math.e)
A_SCALE = HEAD_DIM ** -0.5 * LOG2E
B_SCALE = (MLA_NOPE + MLA_ROPE) ** -0.5 * LOG2E

TILE_PROJ = 512
TILE_MERGE = 256
MERGE_COLS = 256
TILE_EXPERT = 512
TILE_COMBINE = 256
COMBINE_WIN = 768
BAND_TILE = 128
BAND_STEP = 512
FLASH_ROWS = 256
FLASH_KV = 512
FLASH_SUB = 32
FLASH_UNROLL = 8


def _dot(a, b):
    return jnp.dot(a, b, preferred_element_type=F32)


def _dot_nt(a, b):
    return lax.dot_general(a, b, (((1,), (1,)), ((), ())), preferred_element_type=F32)


def _params(*sem):
    return pltpu.CompilerParams(dimension_semantics=sem, vmem_limit_bytes=VMEM_LIMIT)


def _const_spec(shape):
    zeros = (0,) * len(shape)
    return pl.BlockSpec(shape, lambda *_: zeros, pipeline_mode=pl.Buffered(1))


def _smem_spec():
    return pl.BlockSpec(memory_space=pltpu.SMEM)


def _lane_iota(shape):
    return lax.broadcasted_iota(jnp.int32, shape, len(shape) - 1)


def _rope128(x, cos, sin_signed, first_half):
    rot = jnp.where(first_half, pltpu.roll(x, LANE - ROPE_HALF, 1), pltpu.roll(x, ROPE_HALF, 1))
    return x * cos + rot * sin_signed


def _rms(x, gain):
    return x * lax.rsqrt(jnp.mean(x * x, axis=-1, keepdims=True) + EPS) * gain


def _lane_blocks(width):
    return [slice(c * LANE, (c + 1) * LANE) for c in range(width // LANE)]


def _proj_kernel(x_ref, gmix_ref, win_ref, wuq_ref, wukvk_ref, wukvv_ref, gq_ref, gkv_ref, gdq_ref, gdk_ref,
                 g64_ref, oneb_ref, oned_ref, cosb_ref, sinb_ref, cosd_ref, sind_ref,
                 h_ref, qa_ref, ka_ref, va_ref, qb_ref, kb_ref, vb_ref,
                 qc0_ref, kc0_ref, vc0_ref, qc1_ref, kc1_ref, vc1_ref, qc2_ref, kc2_ref, vc2_ref,
                 qd_ref, kd_ref, vd_ref, stage_ref):
    h = _rms(x_ref[...], gmix_ref[...]).astype(BF16)
    h_ref[...] = h

    def seg(lo, width):
        return _dot(h, win_ref[:, lo:lo + width])

    first_half = jnp.bitwise_and(_lane_iota((1, LANE)), 2 * ROPE_HALF - 1) < ROPE_HALF

    qa_ref[...] = (seg(SEG_QA, 256) * A_SCALE).astype(BF16)
    kva = seg(SEG_KA, 256)
    ka_ref[...] = kva[:, :LANE].astype(BF16)
    va_ref[...] = kva[:, LANE:].astype(BF16)

    cosb, sinb = cosb_ref[...], sinb_ref[...]
    cqn = _rms(seg(SEG_CQ, MLA_Q_RANK), gq_ref[...]).astype(BF16)
    qm = _dot(cqn, wuq_ref[...])
    for blk in _lane_blocks(MLA_HEADS * LANE):
        qb_ref[:, blk] = (_rope128(qm[:, blk], cosb, sinb, first_half) * B_SCALE).astype(BF16)
    ckv_kr = seg(SEG_CKV, MLA_KV_RANK + LANE)
    ckvn = _rms(ckv_kr[:, :MLA_KV_RANK], gkv_ref[...]).astype(BF16)
    k_pe = _rope128(ckv_kr[:, MLA_KV_RANK:], cosb, sinb, first_half)
    kn = _dot(ckvn, wukvk_ref[...])
    for blk in _lane_blocks(MLA_HEADS * LANE):
        kb_ref[:, blk] = (kn[:, blk] + k_pe).astype(BF16)
    vb_ref[...] = (_dot(ckvn, wukvv_ref[...]) + oneb_ref[...]).astype(BF16)

    t = x_ref.shape[0]
    c_out = ((qc0_ref, kc0_ref, vc0_ref), (qc1_ref, kc1_ref, vc1_ref), (qc2_ref, kc2_ref, vc2_ref))
    for a, (lo, scale) in enumerate(((SEG_QC, A_SCALE), (SEG_KC, None), (SEG_VC, None))):
        z = seg(lo, 768)
        if scale is not None:
            z = z * scale
        c_out[0][a][...] = z[:, :256].astype(BF16)
        for g in (1, 2):
            dil = C_GROUPS[g][1]
            for c, blk in enumerate(_lane_blocks(256)):
                slot = (a * 2 + (g - 1)) * 2 + c
                stage_ref[slot] = z[:, g * 256 + c * LANE:g * 256 + (c + 1) * LANE]
                for r in range(dil):
                    c_out[g][a][r, :, blk] = stage_ref[slot, pl.ds(r, t // dil, stride=dil), :].astype(BF16)

    cosd, sind = cosd_ref[...], sind_ref[...]
    g64 = g64_ref[...]

    def head_norm(z, gain):
        sq = z * z
        hi = sq.astype(BF16)
        lo = (sq - hi.astype(F32)).astype(BF16)
        ms = _dot(jnp.concatenate([hi, lo], axis=1), g64)
        return z * lax.rsqrt(ms + EPS) * gain

    zq = seg(SEG_QD, 256)
    for blk in _lane_blocks(256):
        y = head_norm(zq[:, blk], gdq_ref[...])
        qd_ref[:, blk] = (_rope128(y, cosd, sind, first_half) * A_SCALE).astype(BF16)
    kvd = seg(SEG_KD, LANE + 256)
    kd_ref[...] = _rope128(head_norm(kvd[:, :LANE], gdk_ref[...]), cosd, sind, first_half).astype(BF16)
    vd_ref[...] = (kvd[:, LANE:] + oned_ref[...]).astype(BF16)


def _proj(x, batch, seq_len, lw, tabs):
    n = x.shape[0]
    t = TILE_PROJ
    per_seq = seq_len // t
    row = lambda w: pl.BlockSpec((t, w), lambda i: (i, 0))
    tab = pl.BlockSpec((t, LANE), lambda i: (i % per_seq, 0))
    flat = lambda w: (row(w), jax.ShapeDtypeStruct((n, w), BF16))

    def planes(dil):
        return (pl.BlockSpec((None, dil, t // dil, 256), lambda i: (i // per_seq, 0, i % per_seq, 0)),
                jax.ShapeDtypeStruct((batch, dil, seq_len // dil, 256), BF16))

    outs = ([flat(D_MODEL), flat(256), flat(128), flat(128), flat(512), flat(512), flat(512)]
            + [flat(256)] * 3 + [planes(C_GROUPS[1][1])] * 3 + [planes(C_GROUPS[2][1])] * 3
            + [flat(256), flat(128), flat(256)])
    return pl.pallas_call(
        _proj_kernel,
        grid=(n // t,),
        in_specs=[row(D_MODEL), _const_spec((1, D_MODEL)), _const_spec((D_MODEL, IN_COLS_PACKED)),
                  _const_spec((MLA_Q_RANK, 512)), _const_spec((MLA_KV_RANK, 512)), _const_spec((MLA_KV_RANK, 512)),
                  _const_spec((1, MLA_Q_RANK)), _const_spec((1, MLA_KV_RANK)), _const_spec((1, LANE)),
                  _const_spec((1, LANE)), _const_spec((2 * LANE, LANE)), _const_spec((1, 512)), _const_spec((1, 256)),
                  tab, tab, tab, tab],
        out_specs=[o[0] for o in outs],
        out_shape=[o[1] for o in outs],
        scratch_shapes=[pltpu.VMEM((12, t, LANE), F32)],
        compiler_params=_params("arbitrary"),
        name="norm_in_proj",
    )(x, lw["g_mix"], lw["w_in"], lw["w_uq"], lw["w_ukv_k"], lw["w_ukv_v"], lw["g_q"], lw["g_kv"], lw["g_dq"],
      lw["g_dk"], tabs["g64"], tabs["ones_b"], tabs["ones_d"], tabs["cos_b"], tabs["sin_b"], tabs["cos_d"],
      tabs["sin_d"])


def _build_bias(tab_ref, bkt_ref, bias_ref, n_heads, band):
    bkt = bkt_ref[...]
    row = lax.broadcasted_iota(jnp.int32, bkt.shape, 0)
    col = lax.broadcasted_iota(jnp.int32, bkt.shape, 1)
    valid = jnp.abs(col - band - row) <= band
    for hh in range(n_heads):
        def body(bb, acc, hh=hh):
            return jnp.where(bkt == bb, tab_ref[bb, hh], acc)
        acc = lax.fori_loop(0, N_BUCKETS, body, jnp.zeros(bkt.shape, F32))
        bias_ref[hh] = jnp.where(valid, acc * LOG2E, NEG)


def _banded_kernel(*refs, band, n_heads, paired_heads, has_sink):
    refs = list(refs)
    tab_ref = refs.pop(0)
    sink_ref = refs.pop(0) if has_sink else None
    bkt_ref, q_ref, kl_ref, kc_ref, kr_ref, vl_ref, vc_ref, vr_ref, o_ref = refs[:9]
    lse_ref = None if has_sink else refs[9]
    bias_ref, s_ref = refs[-2:]
    n = pl.program_id(1)
    last = pl.num_programs(1) - 1

    @pl.when((pl.program_id(0) == 0) & (n == 0))
    def _():
        _build_bias(tab_ref, bkt_ref, bias_ref, n_heads, band)

    t = BAND_TILE
    win = t + 2 * band
    n_sub = q_ref.shape[0] // t
    col = _lane_iota((t, win))
    is_lo = _lane_iota((1, LANE)) < HEAD_DIM
    zero = jnp.zeros((t, LANE), BF16)
    for blk in range(2):
        lanes = slice(blk * LANE, (blk + 1) * LANE)
        kv_lanes = slice(0, LANE) if paired_heads else lanes
        kall = jnp.concatenate([kl_ref[:, kv_lanes], kc_ref[:, kv_lanes], kr_ref[:, kv_lanes]], axis=0)
        for j in range(n_sub):
            qb = q_ref[j * t:(j + 1) * t, lanes]
            for half in range(2):
                qm = jnp.where(is_lo if half == 0 else jnp.logical_not(is_lo), qb, zero)
                s_ref[(blk * n_sub + j) * 2 + half] = _dot_nt(qm, kall[j * t:j * t + win])
    for blk in range(2):
        lanes = slice(blk * LANE, (blk + 1) * LANE)
        kv_lanes = slice(0, LANE) if paired_heads else lanes
        vall = jnp.concatenate([vl_ref[:, kv_lanes], vc_ref[:, kv_lanes], vr_ref[:, kv_lanes]], axis=0)
        for j in range(n_sub):
            rows = slice(j * t, (j + 1) * t)
            vwin = vall[j * t:j * t + win]
            outs, lses = [], []
            for half in range(2):
                head = blk + 2 * half if paired_heads else 2 * blk + half
                s = s_ref[(blk * n_sub + j) * 2 + half] + bias_ref[head]
                if j == 0:
                    s = jnp.where(col >= jnp.where(n > 0, 0, band), s, NEG)
                if j == n_sub - 1:
                    s = jnp.where(col < jnp.where(n < last, win, t + band), s, NEG)
                m = jnp.max(s, axis=-1, keepdims=True)
                if has_sink:
                    sk = sink_ref[head] * LOG2E
                    m = jnp.maximum(m, sk)
                p = jnp.exp2(s - m)
                denom = jnp.sum(p, axis=-1, keepdims=True)
                if has_sink:
                    denom = denom + jnp.exp2(sk - m)
                outs.append(_dot(p.astype(BF16), vwin) / denom)
                if lse_ref is not None:
                    lses.append(m * (1.0 / LOG2E) + jnp.log(denom))
            o_ref[rows, lanes] = jnp.where(is_lo, outs[0], outs[1]).astype(BF16)
            if lse_ref is not None:
                lse_ref[rows, lanes] = jnp.where(is_lo, lses[0], lses[1])


def _banded(q, k, v, table, bkt, band, sink=None):
    s, m, _ = q.shape
    kv_w = k.shape[-1]
    rows = min(BAND_STEP, m)
    per = rows // band
    nh = m // band
    halo = lambda f: pl.BlockSpec((None, band, kv_w), lambda si, ni: (si, f(ni), 0))
    left, right = (lambda ni: jnp.maximum(per * ni - 1, 0)), (lambda ni: jnp.minimum(per * (ni + 1), nh - 1))
    mid = lambda w: pl.BlockSpec((None, rows, w), lambda si, ni: (si, ni, 0))
    win = BAND_TILE + 2 * band
    has_sink = sink is not None
    out_bf = jax.ShapeDtypeStruct((s, m, 256), BF16)
    return pl.pallas_call(
        functools.partial(_banded_kernel, band=band, n_heads=table.shape[1], paired_heads=has_sink,
                          has_sink=has_sink),
        grid=(s, m // rows),
        in_specs=[_smem_spec()] + ([_smem_spec()] if has_sink else [])
                 + [_const_spec((BAND_TILE, win)), mid(256), halo(left), mid(kv_w), halo(right), halo(left),
                    mid(kv_w), halo(right)],
        out_specs=mid(256) if has_sink else [mid(256), mid(256)],
        out_shape=out_bf if has_sink else [out_bf, jax.ShapeDtypeStruct((s, m, 256), F32)],
        scratch_shapes=[pltpu.VMEM((table.shape[1], BAND_TILE, win), F32),
                        pltpu.VMEM((4 * (rows // BAND_TILE), BAND_TILE, win), F32)],
        compiler_params=_params("arbitrary", "arbitrary"),
        name="attn_window_sink" if has_sink else "attn_dilated",
    )(*([table] + ([sink] if has_sink else []) + [bkt, q, k, k, k, v, v, v]))


def _flash_chains(qs, k_refs, v_refs, s_ref, p_ref, m_ref, a_ref, acc_ref):
    nc = len(qs)
    rows = qs[0].shape[0]
    n_chunks = k_refs[0].shape[0] // FLASH_KV
    unroll = min(FLASH_UNROLL, n_chunks)
    ncol = FLASH_KV // LANE

    def scores(slot, chunk):
        start = pl.multiple_of(chunk * FLASH_KV, FLASH_KV)
        for c in range(nc):
            s_ref[slot, c] = _dot_nt(qs[c], k_refs[c][pl.ds(start, FLASH_KV), :])

    def consume(slot, chunk):
        start = pl.multiple_of(chunk * FLASH_KV, FLASH_KV)
        for c in range(nc):
            for r0 in range(0, rows, FLASH_SUB):
                rs = slice(r0, r0 + FLASH_SUB)
                sb = [s_ref[slot, c, rs, j * LANE:(j + 1) * LANE] for j in range(ncol)]
                smax = functools.reduce(jnp.maximum, sb)
                m_old = m_ref[c, rs, :]
                m_new = jnp.maximum(m_old, jnp.max(smax, axis=-1, keepdims=True))
                m_ref[c, rs, :] = m_new
                a_ref[c, rs, :] = jnp.exp2(m_old - m_new)
                for j in range(ncol):
                    p_ref[c, rs, j * LANE:(j + 1) * LANE] = jnp.exp2(sb[j] - m_new).astype(BF16)
            acc_ref[c] = a_ref[c] * acc_ref[c] + _dot(p_ref[c], v_refs[c][pl.ds(start, FLASH_KV), :])

    m_ref[...] = jnp.full(m_ref.shape, NEG, F32)
    acc_ref[...] = jnp.zeros(acc_ref.shape, F32)
    scores(0, 0)

    def trip(i, carry):
        base = i * unroll
        for u in range(unroll):
            scores((u + 1) % 2, base + u + 1)
            consume(u % 2, base + u)
        return carry

    lax.fori_loop(0, n_chunks // unroll - 1, trip, 0)
    base = n_chunks - unroll
    for u in range(unroll):
        if u + 1 < unroll:
            scores((u + 1) % 2, base + u + 1)
        consume(u % 2, base + u)
    return [acc_ref[c] / pltpu.roll(acc_ref[c], HEAD_DIM, 1) for c in range(nc)]


def _flash_scratch(nc, rows):
    stat = pltpu.VMEM((nc, rows, LANE), F32)
    return [pltpu.VMEM((2, nc, rows, FLASH_KV), F32), pltpu.VMEM((nc, rows, FLASH_KV), BF16), stat, stat, stat]


def _flash_b_kernel(qlo_ref, qhi_ref, klo_ref, khi_ref, vlo_ref, vhi_ref, o_ref, *scratch):
    o_lo, o_hi = _flash_chains([qlo_ref[...], qhi_ref[...]], [klo_ref, khi_ref], [vlo_ref, vhi_ref], *scratch)
    is_lo = _lane_iota((1, LANE)) < HEAD_DIM
    o_ref[...] = jnp.where(is_lo, o_lo, o_hi).astype(BF16)


def _flash_b(q, k, v):
    b, l, _ = q.shape
    tq = FLASH_ROWS
    qs = lambda off: pl.BlockSpec((None, tq, LANE), lambda bi, pi, qi: (bi, qi, 2 * pi + off))
    ks = lambda off: pl.BlockSpec((None, l, LANE), lambda bi, pi, qi: (bi, 0, 2 * pi + off))
    return pl.pallas_call(
        _flash_b_kernel,
        grid=(b, 2, l // tq),
        in_specs=[qs(0), qs(1), ks(0), ks(1), ks(0), ks(1)],
        out_specs=pl.BlockSpec((None, tq, LANE), lambda bi, pi, qi: (bi, qi, pi)),
        out_shape=jax.ShapeDtypeStruct((b, l, 256), BF16),
        scratch_shapes=_flash_scratch(2, FLASH_ROWS),
        compiler_params=_params("arbitrary", "arbitrary", "arbitrary"),
        name="attn_dense_mla",
    )(q, q, k, k, v, v)


def _flash_d_kernel(q_ref, k_ref, vlo_ref, vhi_ref, o_ref, *scratch):
    q = q_ref[...]
    tq = q.shape[0]
    is_lo = _lane_iota((1, LANE)) < HEAD_DIM
    zero = jnp.zeros((tq, LANE), BF16)
    qb = [q[:, :LANE], q[:, LANE:]]
    qs = [jnp.concatenate([jnp.where(keep, qb[0], zero), jnp.where(keep, qb[1], zero)], axis=0)
          for keep in (is_lo, jnp.logical_not(is_lo))]
    o_lo, o_hi = _flash_chains(qs, [k_ref, k_ref], [vlo_ref, vhi_ref], *scratch)
    o_ref[:, :LANE] = jnp.where(is_lo, o_lo[:tq], o_hi[:tq]).astype(BF16)
    o_ref[:, LANE:] = jnp.where(is_lo, o_lo[tq:], o_hi[tq:]).astype(BF16)


def _flash_d(q, k, v):
    b, l, _ = q.shape
    tq = FLASH_ROWS // 2
    whole = lambda blk: pl.BlockSpec((None, l, LANE), lambda bi, qi: (bi, 0, blk))
    tile = pl.BlockSpec((None, tq, 256), lambda bi, qi: (bi, qi, 0))
    return pl.pallas_call(
        _flash_d_kernel,
        grid=(b, l // tq),
        in_specs=[tile, whole(0), whole(0), whole(1)],
        out_specs=tile,
        out_shape=jax.ShapeDtypeStruct((b, l, 256), BF16),
        scratch_shapes=_flash_scratch(2, FLASH_ROWS),
        compiler_params=_params("arbitrary", "arbitrary"),
        name="attn_dense_axial",
    )(q, k, v, v)


def _merge_kernel(x_ref, h_ref, oa_ref, ob_ref, oc0_ref, l0_ref, oc1_ref, l1_ref, oc2_ref, l2_ref, od_ref,
                  wg_ref, wb_ref, wo_ref, gffn_ref, wr_ref, xo_ref, h2_ref, aff_ref, stage_ref, merged_ref):
    h = h_ref[...]
    t = h.shape[0]

    def token_order(ref, dil, slot):
        cols = []
        for c, blk in enumerate(_lane_blocks(256)):
            for r in range(dil):
                stage_ref[slot + c, pl.ds(r, t // dil, stride=dil), :] = ref[r, :, blk].astype(F32)
            cols.append(stage_ref[slot + c])
        return jnp.concatenate(cols, axis=1)

    o = [oc0_ref[...].astype(F32), token_order(oc1_ref, C_GROUPS[1][1], 0), token_order(oc2_ref, C_GROUPS[2][1], 2)]
    lse = [l0_ref[...], token_order(l1_ref, C_GROUPS[1][1], 4), token_order(l2_ref, C_GROUPS[2][1], 6)]
    mx = jnp.maximum(jnp.maximum(lse[0], lse[1]), lse[2])
    e = [jnp.exp(l - mx) for l in lse]
    den = e[0] + e[1] + e[2]
    oc = ((e[0] / den) * o[0] + (e[1] / den) * o[1] + (e[2] / den) * o[2]).astype(BF16)
    branches = (oa_ref[...], ob_ref[...], oc, od_ref[...])
    for cols in (slice(c, c + MERGE_COLS) for c in range(0, D_MODEL, MERGE_COLS)):
        merged = None
        for bi in (0, 1, 3, 2):
            term = jax.nn.sigmoid(_dot(h, wg_ref[bi, :, cols])) * _dot(branches[bi], wb_ref[bi, :, cols])
            merged = term if merged is None else merged + term
        merged_ref[:, cols] = merged.astype(BF16)
    xn = x_ref[...] + _dot(merged_ref[...], wo_ref[...])
    xo_ref[...] = xn
    hn = _rms(xn, gffn_ref[...]).astype(BF16)
    h2_ref[...] = hn
    logits = jnp.where(_lane_iota((1, LANE)) < N_EXPERTS, _dot(hn, wr_ref[...]), NEG)
    ex = jnp.exp(logits - jnp.max(logits, axis=-1, keepdims=True))
    aff_ref[...] = ex / jnp.sum(ex, axis=-1, keepdims=True)


def _merge(x, h, oa, ob, oc, lse, od, seq_len, lw):
    n = x.shape[0]
    t = TILE_MERGE
    per_seq = seq_len // t
    row = lambda w: pl.BlockSpec((t, w), lambda i: (i, 0))
    planes = lambda dil: pl.BlockSpec((None, dil, t // dil, 256), lambda i: (i // per_seq, 0, i % per_seq, 0))
    p1, p2 = planes(C_GROUPS[1][1]), planes(C_GROUPS[2][1])
    return pl.pallas_call(
        _merge_kernel,
        grid=(n // t,),
        in_specs=[row(D_MODEL), row(D_MODEL), row(BRANCH_W), row(BRANCH_W), row(BRANCH_W), row(BRANCH_W),
                  p1, p1, p2, p2, row(BRANCH_W),
                  _const_spec((N_BRANCHES, D_MODEL, D_MODEL)), _const_spec((N_BRANCHES, BRANCH_W, D_MODEL)),
                  _const_spec((D_MODEL, D_MODEL)), _const_spec((1, D_MODEL)), _const_spec((D_MODEL, LANE))],
        out_specs=[row(D_MODEL), row(D_MODEL), row(LANE)],
        out_shape=[jax.ShapeDtypeStruct((n, D_MODEL), F32), jax.ShapeDtypeStruct((n, D_MODEL), BF16),
                   jax.ShapeDtypeStruct((n, LANE), F32)],
        scratch_shapes=[pltpu.VMEM((8, t, LANE), F32), pltpu.VMEM((t, D_MODEL), BF16)],
        compiler_params=_params("arbitrary"),
        name="gated_merge_router",
    )(x, h, oa, ob, oc[0], lse[0], oc[1], lse[1], oc[2], lse[2], od,
      lw["w_gate"], lw["w_branch"], lw["w_out"], lw["g_ffn"], lw["w_router"])


def _expert_kernel(xe_ref, gate_ref, wg_ref, wu_ref, wd_ref, ye_ref):
    xe = xe_ref[...]
    he = (jax.nn.silu(_dot(xe, wg_ref[...])) * _dot(xe, wu_ref[...])).astype(BF16)
    ye_ref[...] = _dot(he, wd_ref[...]) * gate_ref[...]


def _experts(xe, gate, lw):
    e, cap, d = xe.shape
    t = min(TILE_EXPERT, cap)
    ff = lw["w_exp_gate"].shape[-1]
    wspec = lambda a, b: pl.BlockSpec((None, a, b), lambda ei, ti: (ei, 0, 0))
    return pl.pallas_call(
        _expert_kernel,
        grid=(e, cap // t),
        in_specs=[pl.BlockSpec((None, t, d), lambda ei, ti: (ei, ti, 0)),
                  pl.BlockSpec((None, t, 1), lambda ei, ti: (ei, ti, 0)),
                  wspec(d, ff), wspec(d, ff), wspec(ff, d)],
        out_specs=pl.BlockSpec((None, t, d), lambda ei, ti: (ei, ti, 0)),
        out_shape=jax.ShapeDtypeStruct((e, cap, d), F32),
        compiler_params=_params("arbitrary", "arbitrary"),
        name="expert_swiglu",
    )(xe, gate, lw["w_exp_gate"], lw["w_exp_up"], lw["w_exp_down"])


def _combine_kernel(starts_ref, x_ref, tok_hbm, z_hbm, *refs, final_norm):
    gain_ref = refs[0] if final_norm else None
    o_ref, tok_buf, z_buf, acc_ref, slot_ref, sem = refs[-6:]
    b = pl.program_id(0)
    last = pl.num_programs(0) - 1
    t = x_ref.shape[0]
    w = COMBINE_WIN

    def span(blk):
        base = lax.shift_left(lax.shift_right_logical(starts_ref[blk], 7), 7)
        return base, lax.div(starts_ref[blk + 1] - base + (w - 1), w)

    def copies(row0, slot):
        row0 = pl.multiple_of(row0, LANE)
        return (pltpu.make_async_copy(tok_hbm.at[:, pl.ds(row0, w)], tok_buf.at[slot], sem.at[0, slot]),
                pltpu.make_async_copy(z_hbm.at[pl.ds(row0, w), :], z_buf.at[slot], sem.at[1, slot]))

    def start(row0, slot):
        for c in copies(row0, slot):
            c.start()

    def start_next_block(slot):
        @pl.when(b < last)
        def _():
            next_base, next_win = span(b + 1)

            @pl.when(next_win > 0)
            def _():
                start(next_base, slot)

    base, n_win = span(b)

    @pl.when(b == 0)
    def _():
        slot_ref[0] = 0

        @pl.when(n_win > 0)
        def _():
            start(base, 0)

    first = slot_ref[0]
    acc_ref[...] = jnp.zeros(acc_ref.shape, F32)
    ids = b * t + lax.broadcasted_iota(jnp.int32, (t, 1), 0)

    def window(i, carry):
        slot = lax.rem(first + i, 2)
        for c in copies(base + i * w, slot):
            c.wait()

        @pl.when(i + 1 < n_win)
        def _():
            start(base + (i + 1) * w, 1 - slot)

        @pl.when(i + 1 == n_win)
        def _():
            start_next_block(1 - slot)
            slot_ref[0] = 1 - slot

        onehot = jnp.where(tok_buf[slot] == ids, 1.0, 0.0).astype(BF16)
        z = z_buf[slot]
        hi = z.astype(BF16)
        lo = (z - hi.astype(F32)).astype(BF16)
        acc_ref[...] += _dot(onehot, hi) + _dot(onehot, lo)
        return carry

    lax.fori_loop(0, n_win, window, 0)

    @pl.when(n_win == 0)
    def _():
        start_next_block(first)

    out = x_ref[...] + acc_ref[...]
    o_ref[...] = _rms(out, gain_ref[...]) if final_norm else out


def _combine(x, tok_sorted, z, starts, final_gain=None):
    n, d = x.shape
    t = TILE_COMBINE
    w = COMBINE_WIN
    final_norm = final_gain is not None
    return pl.pallas_call(
        functools.partial(_combine_kernel, final_norm=final_norm),
        grid_spec=pltpu.PrefetchScalarGridSpec(
            num_scalar_prefetch=1,
            grid=(n // t,),
            in_specs=[pl.BlockSpec((t, d), lambda i, s: (i, 0)), pl.BlockSpec(memory_space=pl.ANY),
                      pl.BlockSpec(memory_space=pl.ANY)]
                     + ([pl.BlockSpec((1, d), lambda i, s: (0, 0))] if final_norm else []),
            out_specs=pl.BlockSpec((t, d), lambda i, s: (i, 0)),
            scratch_shapes=[pltpu.VMEM((2, 1, w), jnp.int32), pltpu.VMEM((2, w, d), F32), pltpu.VMEM((t, d), F32),
                            pltpu.SMEM((1,), jnp.int32), pltpu.SemaphoreType.DMA((2, 2))],
        ),
        out_shape=jax.ShapeDtypeStruct((n, d), F32),
        compiler_params=_params("arbitrary"),
        name="expert_combine",
    )(*([starts, x, tok_sorted, z] + ([final_gain] if final_norm else [])))


def _t5_bucket_np(rel):
    nb = N_BUCKETS // 2
    max_exact = nb // 2
    n = np.abs(rel)
    large = max_exact + (np.log(np.maximum(n, 1).astype(np.float64) / max_exact)
                         / math.log(MAX_DISTANCE / max_exact) * (nb - max_exact)).astype(np.int32)
    large = np.minimum(large, nb - 1)
    return (np.where(rel > 0, nb, 0) + np.where(n < max_exact, n, large)).astype(np.int32)


def _band_buckets(rows, band, dil):
    rel = np.arange(rows + 2 * band, dtype=np.int32)[None, :] - band - np.arange(rows, dtype=np.int32)[:, None]
    return jnp.asarray(_t5_bucket_np(rel * dil))


def _position_tables(seq_len):
    t = jnp.arange(seq_len, dtype=jnp.int32)
    inv = ROPE_THETA ** (-jnp.arange(ROPE_HALF, dtype=F32) / ROPE_HALF)

    def cs(pos):
        ang = pos.astype(F32)[:, None] * inv[None, :]
        return jnp.cos(ang), jnp.sin(ang)

    def group(c, s):
        return jnp.concatenate([c, c], axis=1), jnp.concatenate([-s, s], axis=1)

    ones, zeros = jnp.ones((seq_len, 32), F32), jnp.zeros((seq_len, 32), F32)
    cb, sb = group(*cs(t))
    cr, sr = group(*cs(t // GRID_W))
    cc, sc = group(*cs(t % GRID_W))
    g64 = np.kron(np.eye(2, dtype=np.float32), np.full((HEAD_DIM, HEAD_DIM), 1.0 / HEAD_DIM, np.float32))
    pad_ones = np.tile(np.concatenate([np.zeros(64), np.ones(128), np.zeros(64)]).astype(np.float32), 2)
    return {
        "cos_b": jnp.concatenate([ones, ones, cb, ones], axis=1),
        "sin_b": jnp.concatenate([zeros, zeros, sb, zeros], axis=1),
        "cos_d": jnp.concatenate([cr, cc, cr, cc], axis=1),
        "sin_d": jnp.concatenate([sr, sc, sr, sc], axis=1),
        "g64": jnp.asarray(np.concatenate([g64, g64], axis=0), BF16),
        "ones_b": jnp.asarray(pad_ones.reshape(1, 512)),
        "ones_d": jnp.asarray(pad_ones[:256].reshape(1, 256)),
    }


def _pad_value_heads(w):
    d, heads, _ = w.shape
    z = jnp.zeros((d, heads // 2, HEAD_DIM), w.dtype)
    return jnp.stack([w[:, 0::2], z, z, w[:, 1::2]], axis=2).reshape(d, heads * LANE)


def _pack_layer(l, p):
    w_in = p["w_in"][l]
    split = np.cumsum([0, 256, 128, 128, 256, 128, 32, 768, 768, 768, 256, 128, 128])
    qa, ka, va, cq, ckv, kr, qc, kc, vc, qd, kd, vd = [w_in[:, split[i]:split[i + 1]] for i in range(12)]

    def pair_heads(w):
        d = w.shape[0]
        return w.reshape(d, 4, HEAD_DIM)[:, jnp.array([0, 2, 1, 3])].reshape(d, 256)

    z = lambda w: jnp.zeros((w_in.shape[0], w), F32)
    packed = jnp.concatenate([pair_heads(qa), ka, va, cq, ckv, z(64), kr, z(32), qc, kc, vc, pair_heads(qd), kd,
                              _pad_value_heads(vd.reshape(-1, 2, HEAD_DIM))], axis=1)
    uq = p["mla_w_uq"][l].reshape(MLA_Q_RANK, MLA_HEADS, MLA_NOPE + MLA_ROPE)
    uq = jnp.pad(uq, ((0, 0), (0, 0), (0, LANE - MLA_NOPE - MLA_ROPE))).reshape(MLA_Q_RANK, MLA_HEADS * LANE)
    ukv = p["mla_w_ukv"][l].reshape(MLA_KV_RANK, MLA_HEADS, MLA_NOPE + MLA_V)
    ukv_k = jnp.pad(ukv[:, :, :MLA_NOPE], ((0, 0), (0, 0), (0, LANE - MLA_NOPE))).reshape(MLA_KV_RANK, MLA_HEADS * LANE)
    ukv_v = _pad_value_heads(ukv[:, :, MLA_NOPE:])
    wb = p["w_branch"][l]
    pair_rows = lambda w: w.reshape(4, HEAD_DIM, D_MODEL)[jnp.array([0, 2, 1, 3])].reshape(BRANCH_W, D_MODEL)
    wb = jnp.stack([pair_rows(wb[0]), wb[1], wb[2], pair_rows(wb[3])])
    wr = jnp.pad(p["w_router"][l], ((0, 0), (0, LANE - N_EXPERTS)))
    row = lambda g: g.reshape(1, -1).astype(F32)
    return {
        "g_mix": row(p["norm_mix"][l]), "w_in": packed.astype(BF16), "w_uq": uq.astype(BF16),
        "w_ukv_k": ukv_k.astype(BF16), "w_ukv_v": ukv_v.astype(BF16),
        "g_q": row(p["mla_q_norm"][l]), "g_kv": row(p["mla_kv_norm"][l]),
        "g_dq": row(jnp.tile(p["d_q_norm"][l], 2)), "g_dk": row(jnp.tile(p["d_k_norm"][l], 2)),
        "a_sink": p["a_sink"][l].astype(F32),
        "w_gate": p["w_gate"][l].astype(BF16), "w_branch": wb.astype(BF16), "w_out": p["w_out"][l].astype(BF16),
        "g_ffn": row(p["norm_ffn"][l]), "w_router": wr.astype(BF16),
        "w_exp_gate": p["w_exp_gate"][l].astype(BF16), "w_exp_up": p["w_exp_up"][l].astype(BF16),
        "w_exp_down": p["w_exp_down"][l].astype(BF16),
    }


def _layer(x, b, l, lw, tabs, t5_table, final_gain=None):
    n = b * l
    (h, qa, ka, va, qb, kb, vb, qc0, kc0, vc0, qc1, kc1, vc1, qc2, kc2, vc2, qd, kd, vd) = _proj(x, b, l, lw, tabs)
    seq = lambda a: a.reshape(b, l, a.shape[-1])
    flat = lambda a: a.reshape(n, a.shape[-1])

    oa = flat(_banded(seq(qa), seq(ka), seq(va), t5_table[:, :A_HEADS], tabs["bkt_a"], A_WINDOW, lw["a_sink"]))
    ob = flat(_flash_b(seq(qb), seq(kb), seq(vb)))
    od = flat(_flash_d(seq(qd), seq(kd), seq(vd)))

    oc, lse = [], []
    for g, (q, k, v) in enumerate(((seq(qc0), seq(kc0), seq(vc0)), (qc1, kc1, vc1), (qc2, kc2, vc2))):
        dil = C_GROUPS[g][1]
        table = t5_table[:, A_HEADS + g * C_HEADS:A_HEADS + (g + 1) * C_HEADS]
        streams = lambda a: a.reshape(b * dil, l // dil, 256)
        o_g, lse_g = _banded(streams(q), streams(k), streams(v), table, tabs["bkt_c"][g], C_BAND)
        if dil == 1:
            oc.append(flat(o_g))
            lse.append(flat(lse_g))
        else:
            oc.append(o_g.reshape(b, dil, l // dil, 256))
            lse.append(lse_g.reshape(b, dil, l // dil, 256))

    x_mid, h2, aff = _merge(x, h, oa, ob, oc, lse, od, l, lw)

    cap = EC_CAPACITY_FACTOR * n // N_EXPERTS
    gate, idx = lax.top_k(aff[:, :N_EXPERTS].T, cap)
    ye = _experts(h2[idx], gate[..., None], lw)

    pair_tok = idx.reshape(-1)
    order = jnp.argsort(pair_tok).astype(jnp.int32)
    pad = COMBINE_WIN + LANE
    tok_sorted = jnp.concatenate([pair_tok[order], jnp.full((pad,), -1, jnp.int32)])
    z = ye.reshape(-1, D_MODEL)[jnp.concatenate([order, jnp.zeros((pad,), jnp.int32)])]
    edges = jnp.arange(0, n + 1, TILE_COMBINE, dtype=jnp.int32)
    starts = jnp.sum((pair_tok[None, :] < edges[:, None]).astype(jnp.int32), axis=1)
    return _combine(x_mid, tok_sorted[None, :], z, starts, final_gain)


def _trunk(x, layers, t5_table, norm_final):
    b, l, _ = x.shape
    tabs = _position_tables(l)
    tabs["bkt_a"] = _band_buckets(A_WINDOW, A_WINDOW, 1)
    tabs["bkt_c"] = [_band_buckets(2 * C_BAND, C_BAND, dil) for _, dil in C_GROUPS]
    x = x.reshape(b * l, D_MODEL)
    for i, lw in enumerate(layers):
        final_gain = norm_final.reshape(1, -1).astype(F32) if i == len(layers) - 1 else None
        x = _layer(x, b, l, lw, tabs, t5_table, final_gain)
    return x.reshape(b, l, D_MODEL)


def kernel(x_prompt, x_sample, t5_table, norm_mix, w_in, a_sink, mla_q_norm, mla_w_uq, mla_kv_norm, mla_w_ukv,
           d_q_norm, d_k_norm, w_gate, w_branch, w_out, norm_ffn, w_router, w_exp_gate, w_exp_up, w_exp_down,
           norm_final):
    p = dict(norm_mix=norm_mix, w_in=w_in, a_sink=a_sink, mla_q_norm=mla_q_norm, mla_w_uq=mla_w_uq,
             mla_kv_norm=mla_kv_norm, mla_w_ukv=mla_w_ukv, d_q_norm=d_q_norm, d_k_norm=d_k_norm, w_gate=w_gate,
             w_branch=w_branch, w_out=w_out, norm_ffn=norm_ffn, w_router=w_router, w_exp_gate=w_exp_gate,
             w_exp_up=w_exp_up, w_exp_down=w_exp_down)
    layers = [_pack_layer(l, p) for l in range(w_in.shape[0])]
    t5 = t5_table.astype(F32)
    return (_trunk(x_prompt, layers, t5, norm_final), _trunk(x_sample, layers, t5, norm_final))
```

```python
import functools
import math

import numpy as np
import jax
import jax.numpy as jnp
from jax import lax
from jax.experimental import pallas as pl
from jax.experimental.pallas import tpu as pltpu

F32 = jnp.float32
BF16 = jnp.bfloat16

D_MODEL = 1024
HEAD_DIM = 64
A_HEADS = 4
A_WINDOW = 128
MLA_HEADS = 4
MLA_Q_RANK = 256
MLA_KV_RANK = 128
MLA_NOPE = 64
MLA_ROPE = 32
MLA_V = 64
C_HEADS = 4
C_GROUPS = ((128, 1), (512, 4), (2048, 16))
C_BAND = 64
GRID_W = 64
N_BRANCHES = 4
BRANCH_W = 256
N_BUCKETS = 32
MAX_DISTANCE = 1024
N_EXPERTS = 16
EC_CAPACITY_FACTOR = 2
ROPE_THETA = 10000.0
ROPE_HALF = 16
EPS = 1e-6
NEG = -1e30

LANE = 128
VMEM_LIMIT = 56 * 1024 * 1024

SEG_QA, SEG_KA, SEG_VA = 0, 256, 384
SEG_CQ, SEG_CKV, SEG_KR = 512, 768, 896
SEG_QC, SEG_KC, SEG_VC = 1024, 1792, 2560
SEG_QD, SEG_KD, SEG_VD = 3328, 3584, 3712
IN_COLS_PACKED = 3968

LOG2E = math.log2(math.e)
A_SCALE = HEAD_DIM ** -0.5 * LOG2E
B_SCALE = (MLA_NOPE + MLA_ROPE) ** -0.5 * LOG2E

TILE_PROJ = 512
TILE_MERGE = 512
MERGE_ROWS = 256
MERGE_COLS = 256
TILE_EXPERT = 512
TILE_COMBINE = 256
COMBINE_WIN = 768
BAND_TILE = 128
BAND_STEP = 512
FLASH_ROWS = 256
FLASH_KV = 512
FLASH_SUB = 32
FLASH_UNROLL = 8


def _dot(a, b):
    return jnp.dot(a, b, preferred_element_type=F32)


def _dot_nt(a, b):
    return lax.dot_general(a, b, (((1,), (1,)), ((), ())), preferred_element_type=F32)


def _params(*sem):
    return pltpu.CompilerParams(dimension_semantics=sem, vmem_limit_bytes=VMEM_LIMIT)


def _const_spec(shape):
    zeros = (0,) * len(shape)
    return pl.BlockSpec(shape, lambda *_: zeros, pipeline_mode=pl.Buffered(1))


def _smem_spec():
    return pl.BlockSpec(memory_space=pltpu.SMEM)


def _lane_iota(shape):
    return lax.broadcasted_iota(jnp.int32, shape, len(shape) - 1)


def _rope128(x, cos, sin_signed, first_half):
    rot = jnp.where(first_half, pltpu.roll(x, LANE - ROPE_HALF, 1), pltpu.roll(x, ROPE_HALF, 1))
    return x * cos + rot * sin_signed


def _rms(x, gain):
    return x * lax.rsqrt(jnp.mean(x * x, axis=-1, keepdims=True) + EPS) * gain


def _lane_blocks(width):
    return [slice(c * LANE, (c + 1) * LANE) for c in range(width // LANE)]


def _proj_kernel(x_ref, gmix_ref, win_ref, wuq_ref, wukvk_ref, wukvv_ref, gq_ref, gkv_ref, gdq_ref, gdk_ref,
                 g64_ref, oneb_ref, oned_ref, cosb_ref, sinb_ref, cosd_ref, sind_ref,
                 h_ref, qa_ref, ka_ref, va_ref, qb_ref, kb_ref, vb_ref,
                 qc0_ref, kc0_ref, vc0_ref, qc1_ref, kc1_ref, vc1_ref, qc2_ref, kc2_ref, vc2_ref,
                 qd_ref, kd_ref, vd_ref, stage_ref):
    h = _rms(x_ref[...], gmix_ref[...]).astype(BF16)
    h_ref[...] = h

    def seg(lo, width):
        return _dot(h, win_ref[:, lo:lo + width])

    first_half = jnp.bitwise_and(_lane_iota((1, LANE)), 2 * ROPE_HALF - 1) < ROPE_HALF

    qa_ref[...] = (seg(SEG_QA, 256) * A_SCALE).astype(BF16)
    kva = seg(SEG_KA, 256)
    ka_ref[...] = kva[:, :LANE].astype(BF16)
    va_ref[...] = kva[:, LANE:].astype(BF16)

    cosb, sinb = cosb_ref[...], sinb_ref[...]
    cqn = _rms(seg(SEG_CQ, MLA_Q_RANK), gq_ref[...]).astype(BF16)
    qm = _dot(cqn, wuq_ref[...])
    for blk in _lane_blocks(MLA_HEADS * LANE):
        qb_ref[:, blk] = (_rope128(qm[:, blk], cosb, sinb, first_half) * B_SCALE).astype(BF16)
    ckv_kr = seg(SEG_CKV, MLA_KV_RANK + LANE)
    ckvn = _rms(ckv_kr[:, :MLA_KV_RANK], gkv_ref[...]).astype(BF16)
    k_pe = _rope128(ckv_kr[:, MLA_KV_RANK:], cosb, sinb, first_half)
    kn = _dot(ckvn, wukvk_ref[...])
    for blk in _lane_blocks(MLA_HEADS * LANE):
        kb_ref[:, blk] = (kn[:, blk] + k_pe).astype(BF16)
    vb_ref[...] = (_dot(ckvn, wukvv_ref[...]) + oneb_ref[...]).astype(BF16)

    t = x_ref.shape[0]
    c_out = ((qc0_ref, kc0_ref, vc0_ref), (qc1_ref, kc1_ref, vc1_ref), (qc2_ref, kc2_ref, vc2_ref))
    for a, (lo, scale) in enumerate(((SEG_QC, A_SCALE), (SEG_KC, None), (SEG_VC, None))):
        z = seg(lo, 768)
        if scale is not None:
            z = z * scale
        c_out[0][a][...] = z[:, :256].astype(BF16)
        for g in (1, 2):
            dil = C_GROUPS[g][1]
            for c, blk in enumerate(_lane_blocks(256)):
                slot = (a * 2 + (g - 1)) * 2 + c
                stage_ref[slot] = z[:, g * 256 + c * LANE:g * 256 + (c + 1) * LANE]
                for r in range(dil):
                    c_out[g][a][r, :, blk] = stage_ref[slot, pl.ds(r, t // dil, stride=dil), :].astype(BF16)

    cosd, sind = cosd_ref[...], sind_ref[...]
    g64 = g64_ref[...]

    def head_norm(z, gain):
        sq = z * z
        hi = sq.astype(BF16)
        lo = (sq - hi.astype(F32)).astype(BF16)
        ms = _dot(jnp.concatenate([hi, lo], axis=1), g64)
        return z * lax.rsqrt(ms + EPS) * gain

    zq = seg(SEG_QD, 256)
    for blk in _lane_blocks(256):
        y = head_norm(zq[:, blk], gdq_ref[...])
        qd_ref[:, blk] = (_rope128(y, cosd, sind, first_half) * A_SCALE).astype(BF16)
    kvd = seg(SEG_KD, LANE + 256)
    kd_ref[...] = _rope128(head_norm(kvd[:, :LANE], gdk_ref[...]), cosd, sind, first_half).astype(BF16)
    vd_ref[...] = (kvd[:, LANE:] + oned_ref[...]).astype(BF16)


def _proj(x, batch, seq_len, lw, tabs):
    n = x.shape[0]
    t = TILE_PROJ
    per_seq = seq_len // t
    row = lambda w: pl.BlockSpec((t, w), lambda i: (i, 0))
    tab = pl.BlockSpec((t, LANE), lambda i: (i % per_seq, 0))
    flat = lambda w: (row(w), jax.ShapeDtypeStruct((n, w), BF16))

    def planes(dil):
        return (pl.BlockSpec((None, dil, t // dil, 256), lambda i: (i // per_seq, 0, i % per_seq, 0)),
                jax.ShapeDtypeStruct((batch, dil, seq_len // dil, 256), BF16))

    outs = ([flat(D_MODEL), flat(256), flat(128), flat(128), flat(512), flat(512), flat(512)]
            + [flat(256)] * 3 + [planes(C_GROUPS[1][1])] * 3 + [planes(C_GROUPS[2][1])] * 3
            + [flat(256), flat(128), flat(256)])
    return pl.pallas_call(
        _proj_kernel,
        grid=(n // t,),
        in_specs=[row(D_MODEL), _const_spec((1, D_MODEL)), _const_spec((D_MODEL, IN_COLS_PACKED)),
                  _const_spec((MLA_Q_RANK, 512)), _const_spec((MLA_KV_RANK, 512)), _const_spec((MLA_KV_RANK, 512)),
                  _const_spec((1, MLA_Q_RANK)), _const_spec((1, MLA_KV_RANK)), _const_spec((1, LANE)),
                  _const_spec((1, LANE)), _const_spec((2 * LANE, LANE)), _const_spec((1, 512)), _const_spec((1, 256)),
                  tab, tab, tab, tab],
        out_specs=[o[0] for o in outs],
        out_shape=[o[1] for o in outs],
        scratch_shapes=[pltpu.VMEM((12, t, LANE), F32)],
        compiler_params=_params("arbitrary"),
        name="norm_in_proj",
    )(x, lw["g_mix"], lw["w_in"], lw["w_uq"], lw["w_ukv_k"], lw["w_ukv_v"], lw["g_q"], lw["g_kv"], lw["g_dq"],
      lw["g_dk"], tabs["g64"], tabs["ones_b"], tabs["ones_d"], tabs["cos_b"], tabs["sin_b"], tabs["cos_d"],
      tabs["sin_d"])


def _build_bias(tab_ref, bkt_ref, bias_ref, n_heads, band):
    bkt = bkt_ref[...]
    row = lax.broadcasted_iota(jnp.int32, bkt.shape, 0)
    col = lax.broadcasted_iota(jnp.int32, bkt.shape, 1)
    valid = jnp.abs(col - band - row) <= band
    for hh in range(n_heads):
        def body(bb, acc, hh=hh):
            return jnp.where(bkt == bb, tab_ref[bb, hh], acc)
        acc = lax.fori_loop(0, N_BUCKETS, body, jnp.zeros(bkt.shape, F32))
        bias_ref[hh] = jnp.where(valid, acc * LOG2E, NEG)


def _banded_kernel(*refs, band, n_heads, paired_heads, has_sink):
    refs = list(refs)
    tab_ref = refs.pop(0)
    sink_ref = refs.pop(0) if has_sink else None
    bkt_ref, q_ref, kl_ref, kc_ref, kr_ref, vl_ref, vc_ref, vr_ref, o_ref = refs[:9]
    lse_ref = None if has_sink else refs[9]
    bias_ref, s_ref = refs[-2:]
    n = pl.program_id(1)
    last = pl.num_programs(1) - 1

    @pl.when((pl.program_id(0) == 0) & (n == 0))
    def _():
        _build_bias(tab_ref, bkt_ref, bias_ref, n_heads, band)

    t = BAND_TILE
    win = t + 2 * band
    n_sub = q_ref.shape[0] // t
    col = _lane_iota((t, win))
    is_lo = _lane_iota((1, LANE)) < HEAD_DIM
    zero = jnp.zeros((t, LANE), BF16)
    for blk in range(2):
        lanes = slice(blk * LANE, (blk + 1) * LANE)
        kv_lanes = slice(0, LANE) if paired_heads else lanes
        kall = jnp.concatenate([kl_ref[:, kv_lanes], kc_ref[:, kv_lanes], kr_ref[:, kv_lanes]], axis=0)
        for j in range(n_sub):
            qb = q_ref[j * t:(j + 1) * t, lanes]
            for half in range(2):
                qm = jnp.where(is_lo if half == 0 else jnp.logical_not(is_lo), qb, zero)
                s_ref[(blk * n_sub + j) * 2 + half] = _dot_nt(qm, kall[j * t:j * t + win])
    for blk in range(2):
        lanes = slice(blk * LANE, (blk + 1) * LANE)
        kv_lanes = slice(0, LANE) if paired_heads else lanes
        vall = jnp.concatenate([vl_ref[:, kv_lanes], vc_ref[:, kv_lanes], vr_ref[:, kv_lanes]], axis=0)
        for j in range(n_sub):
            rows = slice(j * t, (j + 1) * t)
            vwin = vall[j * t:j * t + win]
            outs, lses = [], []
            for half in range(2):
                head = blk + 2 * half if paired_heads else 2 * blk + half
                s = s_ref[(blk * n_sub + j) * 2 + half] + bias_ref[head]
                if j == 0:
                    s = jnp.where(col >= jnp.where(n > 0, 0, band), s, NEG)
                if j == n_sub - 1:
                    s = jnp.where(col < jnp.where(n < last, win, t + band), s, NEG)
                m = jnp.max(s, axis=-1, keepdims=True)
                if has_sink:
                    sk = sink_ref[head] * LOG2E
                    m = jnp.maximum(m, sk)
                p = jnp.exp2(s - m)
                denom = jnp.sum(p, axis=-1, keepdims=True)
                if has_sink:
                    denom = denom + jnp.exp2(sk - m)
                outs.append(_dot(p.astype(BF16), vwin) / denom)
                if lse_ref is not None:
                    lses.append(m * (1.0 / LOG2E) + jnp.log(denom))
            o_ref[rows, lanes] = jnp.where(is_lo, outs[0], outs[1]).astype(BF16)
            if lse_ref is not None:
                lse_ref[rows, lanes] = jnp.where(is_lo, lses[0], lses[1])


def _banded(q, k, v, table, bkt, band, sink=None):
    s, m, _ = q.shape
    kv_w = k.shape[-1]
    rows = min(BAND_STEP, m)
    per = rows // band
    nh = m // band
    halo = lambda f: pl.BlockSpec((None, band, kv_w), lambda si, ni: (si, f(ni), 0))
    left, right = (lambda ni: jnp.maximum(per * ni - 1, 0)), (lambda ni: jnp.minimum(per * (ni + 1), nh - 1))
    mid = lambda w: pl.BlockSpec((None, rows, w), lambda si, ni: (si, ni, 0))
    win = BAND_TILE + 2 * band
    has_sink = sink is not None
    out_bf = jax.ShapeDtypeStruct((s, m, 256), BF16)
    return pl.pallas_call(
        functools.partial(_banded_kernel, band=band, n_heads=table.shape[1], paired_heads=has_sink,
                          has_sink=has_sink),
        grid=(s, m // rows),
        in_specs=[_smem_spec()] + ([_smem_spec()] if has_sink else [])
                 + [_const_spec((BAND_TILE, win)), mid(256), halo(left), mid(kv_w), halo(right), halo(left),
                    mid(kv_w), halo(right)],
        out_specs=mid(256) if has_sink else [mid(256), mid(256)],
        out_shape=out_bf if has_sink else [out_bf, jax.ShapeDtypeStruct((s, m, 256), F32)],
        scratch_shapes=[pltpu.VMEM((table.shape[1], BAND_TILE, win), F32),
                        pltpu.VMEM((4 * (rows // BAND_TILE), BAND_TILE, win), F32)],
        compiler_params=_params("arbitrary", "arbitrary"),
        name="attn_window_sink" if has_sink else "attn_dilated",
    )(*([table] + ([sink] if has_sink else []) + [bkt, q, k, k, k, v, v, v]))


def _flash_chains(qs, k_refs, v_refs, s_ref, p_ref, m_ref, a_ref, acc_ref):
    nc = len(qs)
    rows = qs[0].shape[0]
    n_chunks = k_refs[0].shape[0] // FLASH_KV
    unroll = min(FLASH_UNROLL, n_chunks)
    ncol = FLASH_KV // LANE

    def scores(slot, chunk):
        start = pl.multiple_of(chunk * FLASH_KV, FLASH_KV)
        for c in range(nc):
            s_ref[slot, c] = _dot_nt(qs[c], k_refs[c][pl.ds(start, FLASH_KV), :])

    def consume(slot, chunk):
        start = pl.multiple_of(chunk * FLASH_KV, FLASH_KV)
        for c in range(nc):
            for r0 in range(0, rows, FLASH_SUB):
                rs = slice(r0, r0 + FLASH_SUB)
                sb = [s_ref[slot, c, rs, j * LANE:(j + 1) * LANE] for j in range(ncol)]
                smax = functools.reduce(jnp.maximum, sb)
                m_old = m_ref[c, rs, :]
                m_new = jnp.maximum(m_old, jnp.max(smax, axis=-1, keepdims=True))
                m_ref[c, rs, :] = m_new
                a_ref[c, rs, :] = jnp.exp2(m_old - m_new)
                for j in range(ncol):
                    p_ref[c, rs, j * LANE:(j + 1) * LANE] = jnp.exp2(sb[j] - m_new).astype(BF16)
            acc_ref[c] = a_ref[c] * acc_ref[c] + _dot(p_ref[c], v_refs[c][pl.ds(start, FLASH_KV), :])

    m_ref[...] = jnp.full(m_ref.shape, NEG, F32)
    acc_ref[...] = jnp.zeros(acc_ref.shape, F32)
    scores(0, 0)

    def trip(i, carry):
        base = i * unroll
        for u in range(unroll):
            scores((u + 1) % 2, base + u + 1)
            consume(u % 2, base + u)
        return carry

    lax.fori_loop(0, n_chunks // unroll - 1, trip, 0)
    base = n_chunks - unroll
    for u in range(unroll):
        if u + 1 < unroll:
            scores((u + 1) % 2, base + u + 1)
        consume(u % 2, base + u)
    return [acc_ref[c] / pltpu.roll(acc_ref[c], HEAD_DIM, 1) for c in range(nc)]


def _flash_scratch(nc, rows):
    stat = pltpu.VMEM((nc, rows, LANE), F32)
    return [pltpu.VMEM((2, nc, rows, FLASH_KV), F32), pltpu.VMEM((nc, rows, FLASH_KV), BF16), stat, stat, stat]


def _flash_b_kernel(qlo_ref, qhi_ref, klo_ref, khi_ref, vlo_ref, vhi_ref, o_ref, *scratch):
    o_lo, o_hi = _flash_chains([qlo_ref[...], qhi_ref[...]], [klo_ref, khi_ref], [vlo_ref, vhi_ref], *scratch)
    is_lo = _lane_iota((1, LANE)) < HEAD_DIM
    o_ref[...] = jnp.where(is_lo, o_lo, o_hi).astype(BF16)


def _flash_b(q, k, v):
    b, l, _ = q.shape
    tq = FLASH_ROWS
    qs = lambda off: pl.BlockSpec((None, tq, LANE), lambda bi, pi, qi: (bi, qi, 2 * pi + off))
    ks = lambda off: pl.BlockSpec((None, l, LANE), lambda bi, pi, qi: (bi, 0, 2 * pi + off))
    return pl.pallas_call(
        _flash_b_kernel,
        grid=(b, 2, l // tq),
        in_specs=[qs(0), qs(1), ks(0), ks(1), ks(0), ks(1)],
        out_specs=pl.BlockSpec((None, tq, LANE), lambda bi, pi, qi: (bi, qi, pi)),
        out_shape=jax.ShapeDtypeStruct((b, l, 256), BF16),
        scratch_shapes=_flash_scratch(2, FLASH_ROWS),
        compiler_params=_params("arbitrary", "arbitrary", "arbitrary"),
        name="attn_dense_mla",
    )(q, q, k, k, v, v)


def _flash_d_kernel(q_ref, k_ref, vlo_ref, vhi_ref, o_ref, *scratch):
    q = q_ref[...]
    tq = q.shape[0]
    is_lo = _lane_iota((1, LANE)) < HEAD_DIM
    zero = jnp.zeros((tq, LANE), BF16)
    qb = [q[:, :LANE], q[:, LANE:]]
    qs = [jnp.concatenate([jnp.where(keep, qb[0], zero), jnp.where(keep, qb[1], zero)], axis=0)
          for keep in (is_lo, jnp.logical_not(is_lo))]
    o_lo, o_hi = _flash_chains(qs, [k_ref, k_ref], [vlo_ref, vhi_ref], *scratch)
    o_ref[:, :LANE] = jnp.where(is_lo, o_lo[:tq], o_hi[:tq]).astype(BF16)
    o_ref[:, LANE:] = jnp.where(is_lo, o_lo[tq:], o_hi[tq:]).astype(BF16)


def _flash_d(q, k, v):
    b, l, _ = q.shape
    tq = FLASH_ROWS // 2
    whole = lambda blk: pl.BlockSpec((None, l, LANE), lambda bi, qi: (bi, 0, blk))
    tile = pl.BlockSpec((None, tq, 256), lambda bi, qi: (bi, qi, 0))
    return pl.pallas_call(
        _flash_d_kernel,
        grid=(b, l // tq),
        in_specs=[tile, whole(0), whole(0), whole(1)],
        out_specs=tile,
        out_shape=jax.ShapeDtypeStruct((b, l, 256), BF16),
        scratch_shapes=_flash_scratch(2, FLASH_ROWS),
        compiler_params=_params("arbitrary", "arbitrary"),
        name="attn_dense_axial",
    )(q, k, v, v)


def _merge_kernel(x_ref, h_ref, oa_ref, ob_ref, oc0_ref, l0_ref, oc1_ref, l1_ref, oc2_ref, l2_ref, od_ref,
                  wg_ref, wb_ref, wo_ref, gffn_ref, wr_ref, xo_ref, h2_ref, aff_ref, stage_ref, merged_ref):
    for ps in range(h_ref.shape[0] // MERGE_ROWS):
        _merge_rows(ps, x_ref, h_ref, oa_ref, ob_ref, oc0_ref, l0_ref, oc1_ref, l1_ref, oc2_ref, l2_ref, od_ref,
                    wg_ref, wb_ref, wo_ref, gffn_ref, wr_ref, xo_ref, h2_ref, aff_ref, stage_ref, merged_ref)


def _merge_rows(ps, x_ref, h_ref, oa_ref, ob_ref, oc0_ref, l0_ref, oc1_ref, l1_ref, oc2_ref, l2_ref, od_ref,
                wg_ref, wb_ref, wo_ref, gffn_ref, wr_ref, xo_ref, h2_ref, aff_ref, stage_ref, merged_ref):
    t = MERGE_ROWS
    rows = slice(ps * t, (ps + 1) * t)
    h = h_ref[rows, :]

    def token_order(ref, dil, slot):
        cols = []
        for c, blk in enumerate(_lane_blocks(256)):
            buf = 8 * ps + slot + c
            for r in range(dil):
                stage_ref[buf, pl.ds(r, t // dil, stride=dil), :] = (
                    ref[r, ps * t // dil:(ps + 1) * t // dil, blk].astype(F32))
            cols.append(stage_ref[buf])
        return jnp.concatenate(cols, axis=1)

    o = [oc0_ref[rows, :].astype(F32), token_order(oc1_ref, C_GROUPS[1][1], 0),
         token_order(oc2_ref, C_GROUPS[2][1], 2)]
    lse = [l0_ref[rows, :], token_order(l1_ref, C_GROUPS[1][1], 4), token_order(l2_ref, C_GROUPS[2][1], 6)]
    mx = jnp.maximum(jnp.maximum(lse[0], lse[1]), lse[2])
    e = [jnp.exp(l - mx) for l in lse]
    den = e[0] + e[1] + e[2]
    oc = ((e[0] / den) * o[0] + (e[1] / den) * o[1] + (e[2] / den) * o[2]).astype(BF16)
    branches = (oa_ref[rows, :], ob_ref[rows, :], oc, od_ref[rows, :])
    for cols in (slice(c, c + MERGE_COLS) for c in range(0, D_MODEL, MERGE_COLS)):
        merged = None
        for bi in (0, 1, 3, 2):
            term = jax.nn.sigmoid(_dot(h, wg_ref[bi, :, cols])) * _dot(branches[bi], wb_ref[bi, :, cols])
            merged = term if merged is None else merged + term
        merged_ref[rows, cols] = merged.astype(BF16)
    xn = x_ref[rows, :] + _dot(merged_ref[rows, :], wo_ref[...])
    xo_ref[rows, :] = xn
    hn = _rms(xn, gffn_ref[...]).astype(BF16)
    h2_ref[rows, :] = hn
    logits = jnp.where(_lane_iota((1, LANE)) < N_EXPERTS, _dot(hn, wr_ref[...]), NEG)
    ex = jnp.exp(logits - jnp.max(logits, axis=-1, keepdims=True))
    aff_ref[rows, :] = ex / jnp.sum(ex, axis=-1, keepdims=True)


def _merge(x, h, oa, ob, oc, lse, od, seq_len, lw):
    n = x.shape[0]
    t = TILE_MERGE
    per_seq = seq_len // t
    row = lambda w: pl.BlockSpec((t, w), lambda i: (i, 0))
    planes = lambda dil: pl.BlockSpec((None, dil, t // dil, 256), lambda i: (i // per_seq, 0, i % per_seq, 0))
    p1, p2 = planes(C_GROUPS[1][1]), planes(C_GROUPS[2][1])
    return pl.pallas_call(
        _merge_kernel,
        grid=(n // t,),
        in_specs=[row(D_MODEL), row(D_MODEL), row(BRANCH_W), row(BRANCH_W), row(BRANCH_W), row(BRANCH_W),
                  p1, p1, p2, p2, row(BRANCH_W),
                  _const_spec((N_BRANCHES, D_MODEL, D_MODEL)), _const_spec((N_BRANCHES, BRANCH_W, D_MODEL)),
                  _const_spec((D_MODEL, D_MODEL)), _const_spec((1, D_MODEL)), _const_spec((D_MODEL, LANE))],
        out_specs=[row(D_MODEL), row(D_MODEL), row(LANE)],
        out_shape=[jax.ShapeDtypeStruct((n, D_MODEL), F32), jax.ShapeDtypeStruct((n, D_MODEL), BF16),
                   jax.ShapeDtypeStruct((n, LANE), F32)],
        scratch_shapes=[pltpu.VMEM((8 * (t // MERGE_ROWS), MERGE_ROWS, LANE), F32), pltpu.VMEM((t, D_MODEL), BF16)],
        compiler_params=_params("arbitrary"),
        name="gated_merge_router",
    )(x, h, oa, ob, oc[0], lse[0], oc[1], lse[1], oc[2], lse[2], od,
      lw["w_gate"], lw["w_branch"], lw["w_out"], lw["g_ffn"], lw["w_router"])


def _expert_kernel(xe_ref, gate_ref, wg_ref, wu_ref, wd_ref, ye_ref):
    xe = xe_ref[...]
    he = (jax.nn.silu(_dot(xe, wg_ref[...])) * _dot(xe, wu_ref[...])).astype(BF16)
    ye_ref[...] = _dot(he, wd_ref[...]) * gate_ref[...]


def _experts(xe, gate, lw):
    e, cap, d = xe.shape
    t = min(TILE_EXPERT, cap)
    ff = lw["w_exp_gate"].shape[-1]
    wspec = lambda a, b: pl.BlockSpec((None, a, b), lambda ei, ti: (ei, 0, 0))
    return pl.pallas_call(
        _expert_kernel,
        grid=(e, cap // t),
        in_specs=[pl.BlockSpec((None, t, d), lambda ei, ti: (ei, ti, 0)),
                  pl.BlockSpec((None, t, 1), lambda ei, ti: (ei, ti, 0)),
                  wspec(d, ff), wspec(d, ff), wspec(ff, d)],
        out_specs=pl.BlockSpec((None, t, d), lambda ei, ti: (ei, ti, 0)),
        out_shape=jax.ShapeDtypeStruct((e, cap, d), F32),
        compiler_params=_params("arbitrary", "arbitrary"),
        name="expert_swiglu",
    )(xe, gate, lw["w_exp_gate"], lw["w_exp_up"], lw["w_exp_down"])


def _combine_kernel(starts_ref, x_ref, tok_hbm, z_hbm, *refs, final_norm):
    gain_ref = refs[0] if final_norm else None
    o_ref, tok_buf, z_buf, acc_ref, slot_ref, sem = refs[-6:]
    b = pl.program_id(0)
    last = pl.num_programs(0) - 1
    t = x_ref.shape[0]
    w = COMBINE_WIN

    def span(blk):
        base = lax.shift_left(lax.shift_right_logical(starts_ref[blk], 7), 7)
        return base, lax.div(starts_ref[blk + 1] - base + (w - 1), w)

    def copies(row0, slot):
        row0 = pl.multiple_of(row0, LANE)
        return (pltpu.make_async_copy(tok_hbm.at[:, pl.ds(row0, w)], tok_buf.at[slot], sem.at[0, slot]),
                pltpu.make_async_copy(z_hbm.at[pl.ds(row0, w), :], z_buf.at[slot], sem.at[1, slot]))

    def start(row0, slot):
        for c in copies(row0, slot):
            c.start()

    def start_next_block(slot):
        @pl.when(b < last)
        def _():
            next_base, next_win = span(b + 1)

            @pl.when(next_win > 0)
            def _():
                start(next_base, slot)

    base, n_win = span(b)

    @pl.when(b == 0)
    def _():
        slot_ref[0] = 0

        @pl.when(n_win > 0)
        def _():
            start(base, 0)

    first = slot_ref[0]
    acc_ref[...] = jnp.zeros(acc_ref.shape, F32)
    ids = b * t + lax.broadcasted_iota(jnp.int32, (t, 1), 0)

    def window(i, carry):
        slot = lax.rem(first + i, 2)
        for c in copies(base + i * w, slot):
            c.wait()

        @pl.when(i + 1 < n_win)
        def _():
            start(base + (i + 1) * w, 1 - slot)

        @pl.when(i + 1 == n_win)
        def _():
            start_next_block(1 - slot)
            slot_ref[0] = 1 - slot

        onehot = jnp.where(tok_buf[slot] == ids, 1.0, 0.0).astype(BF16)
        z = z_buf[slot]
        hi = z.astype(BF16)
        lo = (z - hi.astype(F32)).astype(BF16)
        acc_ref[...] += _dot(onehot, hi) + _dot(onehot, lo)
        return carry

    lax.fori_loop(0, n_win, window, 0)

    @pl.when(n_win == 0)
    def _():
        start_next_block(first)

    out = x_ref[...] + acc_ref[...]
    o_ref[...] = _rms(out, gain_ref[...]) if final_norm else out


def _combine(x, tok_sorted, z, starts, final_gain=None):
    n, d = x.shape
    t = TILE_COMBINE
    w = COMBINE_WIN
    final_norm = final_gain is not None
    return pl.pallas_call(
        functools.partial(_combine_kernel, final_norm=final_norm),
        grid_spec=pltpu.PrefetchScalarGridSpec(
            num_scalar_prefetch=1,
            grid=(n // t,),
            in_specs=[pl.BlockSpec((t, d), lambda i, s: (i, 0)), pl.BlockSpec(memory_space=pl.ANY),
                      pl.BlockSpec(memory_space=pl.ANY)]
                     + ([pl.BlockSpec((1, d), lambda i, s: (0, 0))] if final_norm else []),
            out_specs=pl.BlockSpec((t, d), lambda i, s: (i, 0)),
            scratch_shapes=[pltpu.VMEM((2, 1, w), jnp.int32), pltpu.VMEM((2, w, d), F32), pltpu.VMEM((t, d), F32),
                            pltpu.SMEM((1,), jnp.int32), pltpu.SemaphoreType.DMA((2, 2))],
        ),
        out_shape=jax.ShapeDtypeStruct((n, d), F32),
        compiler_params=_params("arbitrary"),
        name="expert_combine",
    )(*([starts, x, tok_sorted, z] + ([final_gain] if final_norm else [])))


def _t5_bucket_np(rel):
    nb = N_BUCKETS // 2
    max_exact = nb // 2
    n = np.abs(rel)
    large = max_exact + (np.log(np.maximum(n, 1).astype(np.float64) / max_exact)
                         / math.log(MAX_DISTANCE / max_exact) * (nb - max_exact)).astype(np.int32)
    large = np.minimum(large, nb - 1)
    return (np.where(rel > 0, nb, 0) + np.where(n < max_exact, n, large)).astype(np.int32)


def _band_buckets(rows, band, dil):
    rel = np.arange(rows + 2 * band, dtype=np.int32)[None, :] - band - np.arange(rows, dtype=np.int32)[:, None]
    return jnp.asarray(_t5_bucket_np(rel * dil))


def _position_tables(seq_len):
    t = jnp.arange(seq_len, dtype=jnp.int32)
    inv = ROPE_THETA ** (-jnp.arange(ROPE_HALF, dtype=F32) / ROPE_HALF)

    def cs(pos):
        ang = pos.astype(F32)[:, None] * inv[None, :]
        return jnp.cos(ang), jnp.sin(ang)

    def group(c, s):
        return jnp.concatenate([c, c], axis=1), jnp.concatenate([-s, s], axis=1)

    ones, zeros = jnp.ones((seq_len, 32), F32), jnp.zeros((seq_len, 32), F32)
    cb, sb = group(*cs(t))
    cr, sr = group(*cs(t // GRID_W))
    cc, sc = group(*cs(t % GRID_W))
    g64 = np.kron(np.eye(2, dtype=np.float32), np.full((HEAD_DIM, HEAD_DIM), 1.0 / HEAD_DIM, np.float32))
    pad_ones = np.tile(np.concatenate([np.zeros(64), np.ones(128), np.zeros(64)]).astype(np.float32), 2)
    return {
        "cos_b": jnp.concatenate([ones, ones, cb, ones], axis=1),
        "sin_b": jnp.concatenate([zeros, zeros, sb, zeros], axis=1),
        "cos_d": jnp.concatenate([cr, cc, cr, cc], axis=1),
        "sin_d": jnp.concatenate([sr, sc, sr, sc], axis=1),
        "g64": jnp.asarray(np.concatenate([g64, g64], axis=0), BF16),
        "ones_b": jnp.asarray(pad_ones.reshape(1, 512)),
        "ones_d": jnp.asarray(pad_ones[:256].reshape(1, 256)),
    }


def _pad_value_heads(w):
    d, heads, _ = w.shape
    z = jnp.zeros((d, heads // 2, HEAD_DIM), w.dtype)
    return jnp.stack([w[:, 0::2], z, z, w[:, 1::2]], axis=2).reshape(d, heads * LANE)


def _pack_layer(l, p):
    w_in = p["w_in"][l]
    split = np.cumsum([0, 256, 128, 128, 256, 128, 32, 768, 768, 768, 256, 128, 128])
    qa, ka, va, cq, ckv, kr, qc, kc, vc, qd, kd, vd = [w_in[:, split[i]:split[i + 1]] for i in range(12)]

    def pair_heads(w):
        d = w.shape[0]
        return w.reshape(d, 4, HEAD_DIM)[:, jnp.array([0, 2, 1, 3])].reshape(d, 256)

    z = lambda w: jnp.zeros((w_in.shape[0], w), F32)
    packed = jnp.concatenate([pair_heads(qa), ka, va, cq, ckv, z(64), kr, z(32), qc, kc, vc, pair_heads(qd), kd,
                              _pad_value_heads(vd.reshape(-1, 2, HEAD_DIM))], axis=1)
    uq = p["mla_w_uq"][l].reshape(MLA_Q_RANK, MLA_HEADS, MLA_NOPE + MLA_ROPE)
    uq = jnp.pad(uq, ((0, 0), (0, 0), (0, LANE - MLA_NOPE - MLA_ROPE))).reshape(MLA_Q_RANK, MLA_HEADS * LANE)
    ukv = p["mla_w_ukv"][l].reshape(MLA_KV_RANK, MLA_HEADS, MLA_NOPE + MLA_V)
    ukv_k = jnp.pad(ukv[:, :, :MLA_NOPE], ((0, 0), (0, 0), (0, LANE - MLA_NOPE))).reshape(MLA_KV_RANK, MLA_HEADS * LANE)
    ukv_v = _pad_value_heads(ukv[:, :, MLA_NOPE:])
    wb = p["w_branch"][l]
    pair_rows = lambda w: w.reshape(4, HEAD_DIM, D_MODEL)[jnp.array([0, 2, 1, 3])].reshape(BRANCH_W, D_MODEL)
    wb = jnp.stack([pair_rows(wb[0]), wb[1], wb[2], pair_rows(wb[3])])
    wr = jnp.pad(p["w_router"][l], ((0, 0), (0, LANE - N_EXPERTS)))
    row = lambda g: g.reshape(1, -1).astype(F32)
    return {
        "g_mix": row(p["norm_mix"][l]), "w_in": packed.astype(BF16), "w_uq": uq.astype(BF16),
        "w_ukv_k": ukv_k.astype(BF16), "w_ukv_v": ukv_v.astype(BF16),
        "g_q": row(p["mla_q_norm"][l]), "g_kv": row(p["mla_kv_norm"][l]),
        "g_dq": row(jnp.tile(p["d_q_norm"][l], 2)), "g_dk": row(jnp.tile(p["d_k_norm"][l], 2)),
        "a_sink": p["a_sink"][l].astype(F32),
        "w_gate": p["w_gate"][l].astype(BF16), "w_branch": wb.astype(BF16), "w_out": p["w_out"][l].astype(BF16),
        "g_ffn": row(p["norm_ffn"][l]), "w_router": wr.astype(BF16),
        "w_exp_gate": p["w_exp_gate"][l].astype(BF16), "w_exp_up": p["w_exp_up"][l].astype(BF16),
        "w_exp_down": p["w_exp_down"][l].astype(BF16),
    }


def _layer(x, b, l, lw, tabs, t5_table, final_gain=None):
    n = b * l
    (h, qa, ka, va, qb, kb, vb, qc0, kc0, vc0, qc1, kc1, vc1, qc2, kc2, vc2, qd, kd, vd) = _proj(x, b, l, lw, tabs)
    seq = lambda a: a.reshape(b, l, a.shape[-1])
    flat = lambda a: a.reshape(n, a.shape[-1])

    oa = flat(_banded(seq(qa), seq(ka), seq(va), t5_table[:, :A_HEADS], tabs["bkt_a"], A_WINDOW, lw["a_sink"]))
    ob = flat(_flash_b(seq(qb), seq(kb), seq(vb)))
    od = flat(_flash_d(seq(qd), seq(kd), seq(vd)))

    oc, lse = [], []
    for g, (q, k, v) in enumerate(((seq(qc0), seq(kc0), seq(vc0)), (qc1, kc1, vc1), (qc2, kc2, vc2))):
        dil = C_GROUPS[g][1]
        table = t5_table[:, A_HEADS + g * C_HEADS:A_HEADS + (g + 1) * C_HEADS]
        streams = lambda a: a.reshape(b * dil, l // dil, 256)
        o_g, lse_g = _banded(streams(q), streams(k), streams(v), table, tabs["bkt_c"][g], C_BAND)
        if dil == 1:
            oc.append(flat(o_g))
            lse.append(flat(lse_g))
        else:
            oc.append(o_g.reshape(b, dil, l // dil, 256))
            lse.append(lse_g.reshape(b, dil, l // dil, 256))

    x_mid, h2, aff = _merge(x, h, oa, ob, oc, lse, od, l, lw)

    cap = EC_CAPACITY_FACTOR * n // N_EXPERTS
    gate, idx = lax.top_k(aff[:, :N_EXPERTS].T, cap)
    ye = _experts(h2[idx], gate[..., None], lw)

    pair_tok = idx.reshape(-1)
    order = jnp.argsort(pair_tok).astype(jnp.int32)
    pad = COMBINE_WIN + LANE
    tok_sorted = jnp.concatenate([pair_tok[order], jnp.full((pad,), -1, jnp.int32)])
    z = ye.reshape(-1, D_MODEL)[jnp.concatenate([order, jnp.zeros((pad,), jnp.int32)])]
    edges = jnp.arange(0, n + 1, TILE_COMBINE, dtype=jnp.int32)
    starts = jnp.sum((pair_tok[None, :] < edges[:, None]).astype(jnp.int32), axis=1)
    return _combine(x_mid, tok_sorted[None, :], z, starts, final_gain)


def _trunk(x, layers, t5_table, norm_final):
    b, l, _ = x.shape
    tabs = _position_tables(l)
    tabs["bkt_a"] = _band_buckets(A_WINDOW, A_WINDOW, 1)
    tabs["bkt_c"] = [_band_buckets(2 * C_BAND, C_BAND, dil) for _, dil in C_GROUPS]
    x = x.reshape(b * l, D_MODEL)
    for i, lw in enumerate(layers):
        final_gain = norm_final.reshape(1, -1).astype(F32) if i == len(layers) - 1 else None
        x = _layer(x, b, l, lw, tabs, t5_table, final_gain)
    return x.reshape(b, l, D_MODEL)


def kernel(x_prompt, x_sample, t5_table, norm_mix, w_in, a_sink, mla_q_norm, mla_w_uq, mla_kv_norm, mla_w_ukv,
           d_q_norm, d_k_norm, w_gate, w_branch, w_out, norm_ffn, w_router, w_exp_gate, w_exp_up, w_exp_down,
           norm_final):
    p = dict(norm_mix=norm_mix, w_in=w_in, a_sink=a_sink, mla_q_norm=mla_q_norm, mla_w_uq=mla_w_uq,
             mla_kv_norm=mla_kv_norm, mla_w_ukv=mla_w_ukv, d_q_norm=d_q_norm, d_k_norm=d_k_norm, w_gate=w_gate,
             w_branch=w_branch, w_out=w_out, norm_ffn=norm_ffn, w_router=w_router, w_exp_gate=w_exp_gate,
             w_exp_up=w_exp_up, w_exp_down=w_exp_down)
    layers = [_pack_layer(l, p) for l in range(w_in.shape[0])]
    t5 = t5_table.astype(F32)
    return (_trunk(x_prompt, layers, t5, norm_final), _trunk(x_sample, layers, t5, norm_final))
```
